```python
import math
import jax, jax.numpy as jnp
from jax import lax
import numpy as np

D_MODEL = 1024
BATCH = 1
SEQ = 16384
DEPTH = 2
DEC_BATCH = 128
DEC_SEQ = 8
PAST_LEN = 16384
PAGE_SIZE = 128

HEAD_DIM = 64
MIX_WIDTH = D_MODEL
A_WIDTH = 3 * MIX_WIDTH // 8
A_HEADS = A_WIDTH // HEAD_DIM
B_WIDTH = 3 * MIX_WIDTH // 8
B_HEADS = B_WIDTH // HEAD_DIM
B_KV_HEADS = B_HEADS // 3
B_KV_WIDTH = B_KV_HEADS * HEAD_DIM
C_WIDTH = MIX_WIDTH - A_WIDTH - B_WIDTH
A_PATTERNS = ((128, 1), (512, 4), (2048, 16))
A_MAX_WINDOW = 2048
B_WINDOW = 128
CONV_WIDTH = 3
FF_DIM = 4 * D_MODEL
NUM_BUCKETS = 32
MAX_DISTANCE = 2048
BLOCK = 128
EPS = 1e-6
SCALE = 1.0 / math.sqrt(HEAD_DIM)
_IN_SIZES = (A_WIDTH, A_WIDTH, A_WIDTH, B_WIDTH, B_KV_WIDTH, B_KV_WIDTH, C_WIDTH, C_WIDTH, C_WIDTH)
IN_WIDTH = sum(_IN_SIZES)
IN_SPLITS = tuple(int(s) for s in np.cumsum(_IN_SIZES)[:-1])

kernel_name = 'hybrid_dilated_swa_conv_decoder_step'


def rmsnorm(x, g):
    xf = x.astype(jnp.float32)
    y = xf * lax.rsqrt(jnp.mean(xf * xf, axis=-1, keepdims=True) + EPS)
    return (y * g.astype(jnp.float32)).astype(x.dtype)


def t5_bucket(dist):
    dist = jnp.maximum(dist, 0)
    max_exact = NUM_BUCKETS // 2
    scaled = jnp.log(jnp.maximum(dist, 1).astype(jnp.float32) / max_exact) / math.log(MAX_DISTANCE / max_exact)
    large = max_exact + (scaled * (NUM_BUCKETS - max_exact)).astype(jnp.int32)
    large = jnp.minimum(large, NUM_BUCKETS - 1)
    return jnp.where(dist < max_exact, dist, large)


def strided_band_attention(q, k, v, table, d, wk):
    n_b, s_len, n_h, hd = q.shape
    n_kv = k.shape[2]
    g = n_h // n_kv
    span = d * BLOCK
    s_pad = -(-s_len // span) * span
    sub = s_pad // d
    nb = sub // BLOCK

    def to_sub(t):
        t = jnp.pad(t, ((0, 0), (0, s_pad - s_len), (0, 0), (0, 0)))
        t = t.reshape(n_b, sub, d, t.shape[2], hd).transpose(0, 2, 1, 3, 4)
        return t.reshape(n_b * d, nb, BLOCK, t.shape[3], hd)

    def prev(t):
        return jnp.pad(t, ((0, 0), (1, 0), (0, 0), (0, 0), (0, 0)))[:, :-1]

    qb = to_sub(q).reshape(n_b * d, nb, BLOCK, n_kv, g, hd)
    kb = to_sub(k)
    vb = to_sub(v)
    kk = jnp.concatenate([prev(kb), kb], axis=2)
    vv = jnp.concatenate([prev(vb), vb], axis=2)
    qi = jnp.arange(BLOCK)[:, None]
    kj = jnp.arange(2 * BLOCK)[None, :]
    delta = BLOCK + qi - kj
    in_band = (delta >= 0) & (delta <= wk)
    bias = table[t5_bucket(delta * d)].astype(jnp.float32)
    bias = bias.reshape(BLOCK, 2 * BLOCK, n_kv, g).transpose(2, 3, 0, 1)
    has_prev = (jnp.arange(nb)[:, None, None] > 0) | (kj[None] >= BLOCK)
    mask = in_band[None] & has_prev
    s = jnp.einsum('nbqkgd,nbjkd->nbkgqj', qb, kk, preferred_element_type=jnp.float32) * SCALE + bias
    s = jnp.where(mask[None, :, None, None], s, -jnp.inf)
    m = jnp.max(s, axis=-1, keepdims=True)
    p = jnp.exp(s - m)
    l = jnp.sum(p, axis=-1, keepdims=True)
    o = jnp.einsum('nbkgqj,nbjkd->nbqkgd', p / l, vv.astype(jnp.float32))
    lse = (m + jnp.log(l))[..., 0]
    o = o.reshape(n_b, d, sub, n_h, hd).transpose(0, 2, 1, 3, 4).reshape(n_b, s_pad, n_h, hd)[:, :s_len]
    lse = lse.transpose(0, 1, 4, 2, 3).reshape(n_b, d, sub, n_h).transpose(0, 2, 1, 3)
    lse = lse.reshape(n_b, s_pad, n_h)[:, :s_len]
    return o, lse


def gather_attention(q, k_cat, v_cat, table, d, wk, buf_len):
    n_b, t_len, n_h, hd = q.shape
    n_kv = k_cat.shape[2]
    g = n_h // n_kv
    j = jnp.arange(wk + 1)
    idx = buf_len + jnp.arange(t_len)[:, None] - j[None, :] * d
    valid = idx >= 0
    idx = jnp.maximum(idx, 0)
    kg = jnp.take(k_cat, idx, axis=1)
    vg = jnp.take(v_cat, idx, axis=1)
    bias = table[t5_bucket(j * d)].astype(jnp.float32).reshape(wk + 1, n_kv, g).transpose(1, 2, 0)
    qg = q.reshape(n_b, t_len, n_kv, g, hd)
    s = jnp.einsum('btkgd,btjkd->btkgj', qg, kg, preferred_element_type=jnp.float32) * SCALE + bias
    s = jnp.where(valid[None, :, None, None, :], s, -jnp.inf)
    m = jnp.max(s, axis=-1, keepdims=True)
    p = jnp.exp(s - m)
    l = jnp.sum(p, axis=-1, keepdims=True)
    o = jnp.einsum('btkgj,btjkd->btkgd', p / l, vg.astype(jnp.float32)).reshape(n_b, t_len, n_h, hd)
    lse = (m + jnp.log(l))[..., 0].reshape(n_b, t_len, n_h)
    return o, lse


def combine_by_denominator(outs, lses):
    wts = jax.nn.softmax(jnp.stack(lses, axis=0), axis=0)
    return jnp.sum(wts[..., None] * jnp.stack(outs, axis=0), axis=0)


def project(h, w_in_l):
    n_b, t_len = h.shape[:2]
    proj = jnp.einsum('btd,de->bte', h, w_in_l)
    aq, ak, av, bq, bk, bv, cb, cc, cx = jnp.split(proj, IN_SPLITS, axis=-1)

    def heads(t, n):
        return t.reshape(n_b, t_len, n, HEAD_DIM)

    return (heads(aq, A_HEADS), heads(ak, A_HEADS), heads(av, A_HEADS),
            heads(bq, B_HEADS), heads(bk, B_KV_HEADS), heads(bv, B_KV_HEADS), cb, cc * cx)


def causal_conv(u_ext, w, t_len):
    return sum(w[i] * u_ext[:, i:i + t_len] for i in range(CONV_WIDTH))


def merge_out(o_a, o_b, o_c, w_out_l, dtype):
    n_b, t_len = o_c.shape[:2]
    mix = jnp.concatenate([o_a.reshape(n_b, t_len, A_WIDTH).astype(dtype),
                           o_b.reshape(n_b, t_len, B_WIDTH).astype(dtype),
                           o_c.astype(dtype)], axis=-1)
    return jnp.einsum('btm,md->btd', mix, w_out_l)


def mixer_prompt(h, w_in_l, w_out_l, conv_w_l, sink_l, rel_bias):
    aq, ak, av, bq, bk, bv, cb, u = project(h, w_in_l)
    t_len = h.shape[1]
    outs, lses = [], []
    for w, d in A_PATTERNS:
        o, lse = strided_band_attention(aq, ak, av, rel_bias[:, :A_HEADS], d, w // d)
        outs.append(o)
        lses.append(lse)
    o_a = combine_by_denominator(outs, lses)
    o_b, lse_b = strided_band_attention(bq, bk, bv, rel_bias[:, A_HEADS:], 1, B_WINDOW)
    o_b = o_b * jax.nn.sigmoid(lse_b - sink_l.astype(jnp.float32))[..., None]
    u_ext = jnp.pad(u, ((0, 0), (CONV_WIDTH - 1, 0), (0, 0)))
    o_c = cb * causal_conv(u_ext, conv_w_l, t_len)
    out = merge_out(o_a, o_b, o_c, w_out_l, h.dtype)
    new = (ak[:, -A_MAX_WINDOW:], av[:, -A_MAX_WINDOW:], bk[:, -B_WINDOW:], bv[:, -B_WINDOW:],
           u[:, -(CONV_WIDTH - 1):])
    return out, new


def mixer_sample(h, ck_a, cv_a, ck_b, cv_b, conv_state, w_in_l, w_out_l, conv_w_l, sink_l, rel_bias):
    aq, ak, av, bq, bk, bv, cb, u = project(h, w_in_l)
    t_len = h.shape[1]
    buf_a = ck_a.shape[1]
    buf_b = ck_b.shape[1]
    ka = jnp.concatenate([ck_a.astype(ak.dtype), ak], axis=1)
    va = jnp.concatenate([cv_a.astype(av.dtype), av], axis=1)
    kb = jnp.concatenate([ck_b.astype(bk.dtype), bk], axis=1)
    vb = jnp.concatenate([cv_b.astype(bv.dtype), bv], axis=1)
    outs, lses = [], []
    for w, d in A_PATTERNS:
        o, lse = gather_attention(aq, ka, va, rel_bias[:, :A_HEADS], d, w // d, buf_a)
        outs.append(o)
        lses.append(lse)
    o_a = combine_by_denominator(outs, lses)
    o_b, lse_b = gather_attention(bq, kb, vb, rel_bias[:, A_HEADS:], 1, B_WINDOW, buf_b)
    o_b = o_b * jax.nn.sigmoid(lse_b - sink_l.astype(jnp.float32))[..., None]
    u_ext = jnp.concatenate([conv_state.astype(u.dtype), u], axis=1)
    o_c = cb * causal_conv(u_ext, conv_w_l, t_len)
    out = merge_out(o_a, o_b, o_c, w_out_l, h.dtype)
    new = (ka[:, -A_MAX_WINDOW:], va[:, -A_MAX_WINDOW:], kb[:, -B_WINDOW:], vb[:, -B_WINDOW:],
           u_ext[:, -(CONV_WIDTH - 1):])
    return out, new


def ffn_sublayer(x, g_pre, w_up_l, w_down_l, g_post):
    h = rmsnorm(x, g_pre)
    a = jax.nn.relu(jnp.einsum('btd,df->btf', h, w_up_l))
    return x + rmsnorm(jnp.einsum('btf,fd->btd', a * a, w_down_l), g_post)


def setup_inputs(seed: int = 0) -> dict:
    key = jax.random.key(seed)
    ks = jax.random.split(key, 20)
    f32 = jnp.float32
    buf_a = min(A_MAX_WINDOW, PAST_LEN)
    buf_b = min(B_WINDOW, PAST_LEN)

    def nrm(k, shape, s=1.0):
        return s * jax.random.normal(k, shape, f32)

    return {
        'x_prompt': nrm(ks[0], (BATCH, SEQ, D_MODEL)),
        'x_sample': nrm(ks[1], (DEC_BATCH, DEC_SEQ, D_MODEL)),
        'cache_a_k': nrm(ks[2], (DEPTH, DEC_BATCH, buf_a, A_HEADS, HEAD_DIM)),
        'cache_a_v': nrm(ks[3], (DEPTH, DEC_BATCH, buf_a, A_HEADS, HEAD_DIM)),
        'cache_b_k': nrm(ks[4], (DEPTH, DEC_BATCH, buf_b, B_KV_HEADS, HEAD_DIM)),
        'cache_b_v': nrm(ks[5], (DEPTH, DEC_BATCH, buf_b, B_KV_HEADS, HEAD_DIM)),
        'state_conv': nrm(ks[6], (DEPTH, DEC_BATCH, CONV_WIDTH - 1, C_WIDTH)),
        'w_in': nrm(ks[7], (DEPTH, D_MODEL, IN_WIDTH), D_MODEL ** -0.5),
        'w_out': nrm(ks[8], (DEPTH, MIX_WIDTH, D_MODEL), MIX_WIDTH ** -0.5),
        'conv_w': nrm(ks[9], (DEPTH, CONV_WIDTH, C_WIDTH), 0.5),
        'b_sinks': nrm(ks[10], (DEPTH, B_HEADS)),
        'rel_bias': nrm(ks[11], (NUM_BUCKETS, A_HEADS + B_HEADS), 0.5),
        'g_mix_pre': 1.0 + nrm(ks[12], (DEPTH, D_MODEL), 0.1),
        'g_mix_post': 1.0 + nrm(ks[13], (DEPTH, D_MODEL), 0.1),
        'w_up': nrm(ks[14], (DEPTH, D_MODEL, FF_DIM), D_MODEL ** -0.5),
        'w_down': nrm(ks[15], (DEPTH, FF_DIM, D_MODEL), FF_DIM ** -0.5),
        'g_mlp_pre': 1.0 + nrm(ks[16], (DEPTH, D_MODEL), 0.1),
        'g_mlp_post': 1.0 + nrm(ks[17], (DEPTH, D_MODEL), 0.1),
    }


def reference(x_prompt, x_sample, cache_a_k, cache_a_v, cache_b_k, cache_b_v, state_conv,
              w_in, w_out, conv_w, b_sinks, rel_bias, g_mix_pre, g_mix_post, w_up, w_down,
              g_mlp_pre, g_mlp_post):
    xp = x_prompt
    xs = x_sample
    new_p = [[], [], [], [], []]
    new_s = [[], [], [], [], []]
    for l in range(DEPTH):
        mp, st_p = mixer_prompt(rmsnorm(xp, g_mix_pre[l]), w_in[l], w_out[l], conv_w[l], b_sinks[l], rel_bias)
        xp = xp + rmsnorm(mp, g_mix_post[l])
        xp = ffn_sublayer(xp, g_mlp_pre[l], w_up[l], w_down[l], g_mlp_post[l])
        ms, st_s = mixer_sample(rmsnorm(xs, g_mix_pre[l]), cache_a_k[l], cache_a_v[l], cache_b_k[l],
                                cache_b_v[l], state_conv[l], w_in[l], w_out[l], conv_w[l], b_sinks[l],
                                rel_bias)
        xs = xs + rmsnorm(ms, g_mix_post[l])
        xs = ffn_sublayer(xs, g_mlp_pre[l], w_up[l], w_down[l], g_mlp_post[l])
        for i in range(5):
            new_p[i].append(st_p[i])
            new_s[i].append(st_s[i])
    a_k_p, a_v_p, b_k_p, b_v_p, conv_p = [jnp.stack(t, axis=0) for t in new_p]
    a_k_s, a_v_s, b_k_s, b_v_s, conv_s = [jnp.stack(t, axis=0) for t in new_s]
    return (xp, xs, a_k_p, a_v_p, b_k_p, b_v_p, conv_p, a_k_s, a_v_s, b_k_s, b_v_s, conv_s)
```

```python
import functools
import math

import jax
import jax.numpy as jnp
from jax import lax
from jax.experimental import pallas as pl
from jax.experimental.pallas import tpu as pltpu

HEAD_DIM = 64
HEADS = 6
A_WIDTH = HEADS * HEAD_DIM
B_KV_WIDTH = 2 * HEAD_DIM
C_WIDTH = 256
A_PATTERNS = ((128, 1), (512, 4), (2048, 16))
A_WINDOW = 2048
B_WINDOW = 128
BLOCK = 128
CONV_WIDTH = 3
NUM_BUCKETS = 32
MAX_DISTANCE = 2048
EPS = 1e-6
SCALE = 1.0 / math.sqrt(HEAD_DIM)
NEG = -1e30
LANES = 128
SUBLANES = 8
ROW_TILE = 512
FF_CHUNK = 1024
VMEM_LIMIT = 56 * 1024 * 1024

_IN_SIZES = (A_WIDTH, A_WIDTH, A_WIDTH, A_WIDTH, B_KV_WIDTH, B_KV_WIDTH, C_WIDTH, C_WIDTH, C_WIDTH)
_IN_OFFS = tuple(sum(_IN_SIZES[:i]) for i in range(len(_IN_SIZES) + 1))

f32 = jnp.float32
bf16 = jnp.bfloat16


def _rms(x, g):
    ms = jnp.mean(x * x, axis=-1, keepdims=True)
    return (x * lax.rsqrt(ms + EPS)) * g


def _dot(a, b):
    return jnp.dot(a, b, preferred_element_type=f32)


def _dot_nt(a, b):
    return lax.dot_general(a, b, (((1,), (1,)), ((), ())), preferred_element_type=f32)


def _project(x_ref, g_ref, w_ref):
    hb = _rms(x_ref[...], g_ref[...]).astype(bf16)
    return [_dot(hb, w_ref[:, _IN_OFFS[i]:_IN_OFFS[i + 1]]) for i in range(len(_IN_SIZES))]


def _swap_halves(t):
    return pltpu.roll(t, HEAD_DIM, axis=1)


def _low_half(shape):
    return lax.broadcasted_iota(jnp.int32, shape, 1) < HEAD_DIM


def _prompt_proj_kernel(n_steps, x_ref, g_ref, w_ref, cw_ref,
                        qa_ref, ka_ref, va_ref, qb_ref, kbx_ref, vbx_ref, oc_ref,
                        kat_ref, vat_ref, kbt_ref, vbt_ref, ut_ref, uext_ref):
    t = x_ref.shape[0]
    step = pl.program_id(0)
    aq, ak, av, bq, bk, bv, cb, cc, cx = _project(x_ref, g_ref, w_ref)
    qa_ref[...] = (aq * SCALE).astype(bf16)
    ka_ref[...] = ak.astype(bf16)
    va_ref[...] = av.astype(bf16)
    qb_ref[...] = (bq * SCALE).astype(bf16)
    low = _low_half(bk.shape)

    def widen(kv):
        sw = _swap_halves(kv)
        return jnp.concatenate([jnp.where(low, kv, sw), kv, jnp.where(low, sw, kv)], axis=1)

    kbx_ref[...] = widen(bk).astype(bf16)
    vbx_ref[...] = widen(bv).astype(bf16)

    @pl.when(step >= n_steps - A_WINDOW // t)
    def _():
        kat_ref[...] = ak.T
        vat_ref[...] = av.T

    @pl.when(step == 0)
    def _():
        uext_ref[0:SUBLANES, :] = jnp.zeros((SUBLANES, C_WIDTH), f32)

    u = cc * cx
    uext_ref[SUBLANES:, :] = u
    u1 = uext_ref[pl.ds(SUBLANES - 1, t), :]
    u2 = uext_ref[pl.ds(SUBLANES - 2, t), :]
    cw = cw_ref[...]
    conv = cw[0:1, :] * u2 + cw[1:2, :] * u1 + cw[2:3, :] * u
    oc_ref[...] = (cb * conv).astype(bf16)

    @pl.when(step == n_steps - 1)
    def _():
        kbt_ref[...] = bk[t - B_WINDOW:, :].T
        vbt_ref[...] = bv[t - B_WINDOW:, :].T
        ut_ref[...] = uext_ref[pl.ds(t + SUBLANES - (CONV_WIDTH - 1), CONV_WIDTH - 1), :]

    uext_ref[0:SUBLANES, :] = u[t - SUBLANES:, :]


def _prompt_proj(x, g, w, cw):
    s, d = x.shape
    t = ROW_TILE
    n = s // t
    tail_blocks = A_WINDOW // t
    row = lambda width: pl.BlockSpec((t, width), lambda i: (i, 0))
    const = lambda shape: pl.BlockSpec(shape, lambda i: (0, 0))
    tail = pl.BlockSpec((A_WIDTH, t), lambda i: (0, jnp.maximum(i - (n - tail_blocks), 0)))
    bfo = lambda width: jax.ShapeDtypeStruct((s, width), bf16)
    return pl.pallas_call(
        functools.partial(_prompt_proj_kernel, n),
        grid=(n,),
        in_specs=[row(d), const((1, d)), const(w.shape), const(cw.shape)],
        out_specs=[row(A_WIDTH)] * 6 + [row(C_WIDTH), tail, tail,
                                        const((B_KV_WIDTH, B_WINDOW)), const((B_KV_WIDTH, B_WINDOW)),
                                        const((CONV_WIDTH - 1, C_WIDTH))],
        out_shape=[bfo(A_WIDTH)] * 6 + [bfo(C_WIDTH),
                                        jax.ShapeDtypeStruct((A_WIDTH, A_WINDOW), f32),
                                        jax.ShapeDtypeStruct((A_WIDTH, A_WINDOW), f32),
                                        jax.ShapeDtypeStruct((B_KV_WIDTH, B_WINDOW), f32),
                                        jax.ShapeDtypeStruct((B_KV_WIDTH, B_WINDOW), f32),
                                        jax.ShapeDtypeStruct((CONV_WIDTH - 1, C_WIDTH), f32)],
        scratch_shapes=[pltpu.VMEM((t + SUBLANES, C_WIDTH), f32)],
        compiler_params=pltpu.CompilerParams(dimension_semantics=("arbitrary",),
                                             vmem_limit_bytes=VMEM_LIMIT),
        name="prompt_proj",
    )(x, g, w, cw)


def _band_attn_kernel(gated, *refs):
    if gated:
        sink_ref, q_ref, kp_ref, kc_ref, vp_ref, vc_ref, bias_ref, o_ref = refs
    else:
        q_ref, kp_ref, kc_ref, vp_ref, vc_ref, bias_ref, o_ref, lse_ref = refs
    q = q_ref[...]
    k2 = jnp.concatenate([kp_ref[...], kc_ref[...]], axis=0)
    v2 = jnp.concatenate([vp_ref[...], vc_ref[...]], axis=0)
    low = _low_half((BLOCK, LANES))
    lane = lax.broadcasted_iota(jnp.int32, (BLOCK, LANES), 1)
    zero = jnp.zeros((BLOCK, LANES), bf16)
    lse_tile = jnp.zeros((BLOCK, LANES), f32)
    for p in range(HEADS // 2):
        cols = slice(p * LANES, (p + 1) * LANES)
        qt, kt, vt = q[:, cols], k2[:, cols], v2[:, cols]
        halves = []
        for e in range(2):
            h = 2 * p + e
            qm = jnp.where(low if e == 0 else ~low, qt, zero)
            s = _dot_nt(qm, kt) + bias_ref[h]
            m = jnp.max(s, axis=-1, keepdims=True)
            pe = jnp.exp(s - m)
            l = jnp.sum(pe, axis=-1, keepdims=True)
            o = _dot(pe.astype(bf16), vt) / l
            lse = m + jnp.log(l)
            if gated:
                o = o * jax.nn.sigmoid(lse - sink_ref[h])
            else:
                lse_tile = jnp.where(lane == h, lse, lse_tile)
            halves.append(o)
        o_ref[:, cols] = jnp.where(low, halves[0], halves[1]).astype(bf16)
    if not gated:
        lse_ref[...] = lse_tile


def _band_attention(q, k, v, bias, d, sinks=None):
    s = q.shape[0]
    nb = s // (d * BLOCK)
    view = lambda a, w: a.reshape(s // d, d * w)
    cur = pl.BlockSpec((BLOCK, A_WIDTH), lambda c, j: (j, c))
    prev = pl.BlockSpec((BLOCK, A_WIDTH), lambda c, j: (jnp.maximum(j - 1, 0), c))
    bias_spec = pl.BlockSpec((None, HEADS, BLOCK, 2 * BLOCK), lambda c, j: (jnp.minimum(j, 1), 0, 0, 0))
    gated = sinks is not None
    in_specs = [cur, prev, cur, prev, cur, bias_spec]
    args = [view(q, A_WIDTH), view(k, A_WIDTH), view(k, A_WIDTH), view(v, A_WIDTH), view(v, A_WIDTH), bias]
    out_specs = [cur]
    out_shape = [jax.ShapeDtypeStruct((s // d, d * A_WIDTH), bf16)]
    if gated:
        in_specs = [pl.BlockSpec(memory_space=pltpu.SMEM)] + in_specs
        args = [sinks] + args
    else:
        out_specs.append(pl.BlockSpec((BLOCK, LANES), lambda c, j: (j, c)))
        out_shape.append(jax.ShapeDtypeStruct((s // d, d * LANES), f32))
    outs = pl.pallas_call(
        functools.partial(_band_attn_kernel, gated),
        grid=(d, nb),
        in_specs=in_specs, out_specs=out_specs, out_shape=out_shape,
        compiler_params=pltpu.CompilerParams(dimension_semantics=("arbitrary", "arbitrary"),
                                             vmem_limit_bytes=VMEM_LIMIT),
        name=f"band_attn_d{d}" + ("_gated" if gated else ""),
    )(*args)
    if gated:
        return outs[0].reshape(s, A_WIDTH)
    return outs[0].reshape(s, A_WIDTH), outs[1].reshape(s, LANES)


def _finish_rows(x, mix_b, wo_ref, gpost_ref, gpre2_ref, wup_ref, wdn_ref, gpost2_ref, out_ref):
    x1 = x + _rms(_dot(mix_b, wo_ref[...]), gpost_ref[...])
    hb = _rms(x1, gpre2_ref[...]).astype(bf16)
    ff = wup_ref.shape[1]
    acc = jnp.zeros(x.shape, f32)
    for c in range(ff // FF_CHUNK):
        cols = slice(c * FF_CHUNK, (c + 1) * FF_CHUNK)
        a = jnp.maximum(_dot(hb, wup_ref[:, cols]), 0.0)
        acc = acc + _dot((a * a).astype(bf16), wdn_ref[cols, :])
    out_ref[...] = x1 + _rms(acc, gpost2_ref[...])


def _prompt_finish_kernel(x_ref, o1_ref, o4_ref, o16_ref, l1_ref, l4_ref, l16_ref, ob_ref, oc_ref, exp_ref,
                          wo_ref, gpost_ref, gpre2_ref, wup_ref, wdn_ref, gpost2_ref, out_ref):
    lses = [l1_ref[...], l4_ref[...], l16_ref[...]]
    m = jnp.maximum(jnp.maximum(lses[0], lses[1]), lses[2])
    es = [jnp.exp(l - m) for l in lses]
    tot = es[0] + es[1] + es[2]
    oa = jnp.zeros(o1_ref.shape, f32)
    for e, o_ref in zip(es, (o1_ref, o4_ref, o16_ref)):
        w = e / tot
        w_hi = w.astype(bf16)
        w_lo = (w - w_hi.astype(f32)).astype(bf16)
        wx = _dot(jnp.concatenate([w_hi, w_lo], axis=1), exp_ref[...])
        oa = oa + wx * o_ref[...].astype(f32)
    mix_b = jnp.concatenate([oa.astype(bf16), ob_ref[...], oc_ref[...]], axis=1)
    _finish_rows(x_ref[...], mix_b, wo_ref, gpost_ref, gpre2_ref, wup_ref, wdn_ref, gpost2_ref, out_ref)


def _sample_finish_kernel(x_ref, mix_ref, wo_ref, gpost_ref, gpre2_ref, wup_ref, wdn_ref, gpost2_ref, out_ref):
    _finish_rows(x_ref[...], mix_ref[...].astype(bf16), wo_ref, gpost_ref, gpre2_ref, wup_ref, wdn_ref,
                 gpost2_ref, out_ref)


def _finish(kernel_fn, name, x, row_inputs, const_inputs):
    s, d = x.shape
    t = ROW_TILE
    row = lambda a: pl.BlockSpec((t, a.shape[1]), lambda i: (i, 0))
    const = lambda a: pl.BlockSpec(a.shape, lambda i: (0, 0), pipeline_mode=pl.Buffered(1))
    return pl.pallas_call(
        kernel_fn,
        grid=(s // t,),
        in_specs=[row(a) for a in [x] + row_inputs] + [const(a) for a in const_inputs],
        out_specs=pl.BlockSpec((t, d), lambda i: (i, 0)),
        out_shape=jax.ShapeDtypeStruct((s, d), f32),
        compiler_params=pltpu.CompilerParams(dimension_semantics=("arbitrary",),
                                             vmem_limit_bytes=VMEM_LIMIT),
        name=name,
    )(x, *row_inputs, *const_inputs)


def _sample_proj_kernel(x_ref, g_ref, w_ref, qa_ref, ka_ref, va_ref, qb_ref, kb_ref, vb_ref, cb_ref, u_ref):
    aq, ak, av, bq, bk, bv, cb, cc, cx = _project(x_ref, g_ref, w_ref)
    qa_ref[...] = aq * SCALE
    ka_ref[...] = ak
    va_ref[...] = av
    qb_ref[...] = bq * SCALE
    kb_ref[...] = bk
    vb_ref[...] = bv
    cb_ref[...] = cb
    u_ref[...] = cc * cx


def _sample_proj(x, g, w):
    s, d = x.shape
    t = ROW_TILE
    widths = (A_WIDTH, A_WIDTH, A_WIDTH, A_WIDTH, B_KV_WIDTH, B_KV_WIDTH, C_WIDTH, C_WIDTH)
    row = lambda width: pl.BlockSpec((t, width), lambda i: (i, 0))
    const = lambda shape: pl.BlockSpec(shape, lambda i: (0, 0))
    return pl.pallas_call(
        _sample_proj_kernel,
        grid=(s // t,),
        in_specs=[row(d), const((1, d)), const(w.shape)],
        out_specs=[row(wd) for wd in widths],
        out_shape=[jax.ShapeDtypeStruct((s, wd), f32) for wd in widths],
        compiler_params=pltpu.CompilerParams(dimension_semantics=("arbitrary",),
                                             vmem_limit_bytes=VMEM_LIMIT),
        name="sample_proj",
    )(x, g, w)


def _softmax_rows(s):
    m = jnp.max(s, axis=-1, keepdims=True)
    p = jnp.exp(s - m)
    l = jnp.sum(p, axis=-1, keepdims=True)
    return p, m, l


def _sample_attn_kernel(n_alias, sink_ref, qa_ref, ka_ref, va_ref, qb_ref, kb_ref, vb_ref, cb_ref, u_ref,
                        cak_ref, cav_ref, cbk_ref, cbv_ref, cst_ref, taba_ref, tabb_ref, cw_ref, *rest):
    mix_ref, nak_ref, nav_ref, nbk_ref, nbv_ref, nst_ref, uext_s = rest[n_alias:]
    t = qa_ref.shape[0]
    rows = HEADS * t
    shift_t = t.bit_length() - 1

    def pad_rows(new, at_end):
        z = jnp.zeros((LANES - t, new.shape[1]), f32)
        return jnp.concatenate([z, new] if at_end else [new, z], axis=0)

    def attend(qbd, kt_ref, vt_ref, k_new, v_new, tab_ref):
        buf = kt_ref.shape[1]
        s = jnp.concatenate([_dot(qbd, kt_ref[...].astype(bf16)),
                             _dot_nt(qbd, pad_rows(k_new, False).astype(bf16))], axis=1) + tab_ref[...]
        p, m, l = _softmax_rows(s)
        pb = p.astype(bf16)
        res = _dot_nt(pb[:, :buf], vt_ref[...].astype(bf16)) + _dot(pb[:, buf:], pad_rows(v_new, False).astype(bf16))
        return res / l, m + jnp.log(l)

    def shift(dst_ref, ct_ref, new):
        w, buf = ct_ref.shape
        new_t = jnp.transpose(pad_rows(new, True))
        rolled = pltpu.roll(ct_ref[...], buf - t, axis=1)
        if buf > LANES:
            dst_ref[:, 0:buf - LANES] = rolled[:, 0:buf - LANES]
        lane = lax.broadcasted_iota(jnp.int32, (w, LANES), 1)
        dst_ref[:, buf - LANES:] = jnp.where(lane >= LANES - t, new_t, rolled[:, buf - LANES:])

    ka_new, va_new = ka_ref[...], va_ref[...]
    q6 = jnp.concatenate([qa_ref[...]] * HEADS, axis=0)
    row_head = lax.broadcasted_iota(jnp.int32, (rows, A_WIDTH), 0) >> shift_t
    col_head = lax.broadcasted_iota(jnp.int32, (rows, A_WIDTH), 1) >> (HEAD_DIM.bit_length() - 1)
    own = row_head == col_head
    qbd = jnp.where(own, q6, 0.0).astype(bf16)
    res, _ = attend(qbd, cak_ref, cav_ref, ka_new, va_new, taba_ref)
    res = jnp.where(own, res, 0.0)
    oa = res[0:t, :]
    for h in range(1, HEADS):
        oa = oa + res[h * t:(h + 1) * t, :]
    shift(nak_ref, cak_ref, ka_new)
    shift(nav_ref, cav_ref, va_new)

    kb_new, vb_new = kb_ref[...], vb_ref[...]
    qb = qb_ref[...]
    low = _low_half((t, LANES))
    zero = jnp.zeros((t, LANES), f32)
    t0, t1, t2 = (qb[:, i * LANES:(i + 1) * LANES] for i in range(3))
    qbd_b = jnp.concatenate([
        jnp.where(low, t0, zero), jnp.where(low, _swap_halves(t0), zero), jnp.where(low, t1, zero),
        jnp.where(low, zero, t1), jnp.where(low, zero, _swap_halves(t2)), jnp.where(low, zero, t2)],
        axis=0).astype(bf16)
    resb, lse_b = attend(qbd_b, cbk_ref, cbv_ref, kb_new, vb_new, tabb_ref)
    row_head_b = lax.broadcasted_iota(jnp.int32, (rows, 1), 0) >> shift_t
    sink_col = jnp.zeros((rows, 1), f32)
    for h in range(HEADS):
        sink_col = jnp.where(row_head_b == h, sink_ref[h], sink_col)
    resb = resb * jax.nn.sigmoid(lse_b - sink_col)
    r = [resb[h * t:(h + 1) * t, :] for h in range(HEADS)]
    ob = jnp.concatenate([jnp.where(low, r[0], _swap_halves(r[1])), jnp.where(low, r[2], r[3]),
                          jnp.where(low, _swap_halves(r[4]), r[5])], axis=1)
    shift(nbk_ref, cbk_ref, kb_new)
    shift(nbv_ref, cbv_ref, vb_new)

    u = u_ref[...]
    uext_s[SUBLANES - (CONV_WIDTH - 1):SUBLANES, :] = cst_ref[...]
    uext_s[SUBLANES:, :] = u
    u1 = uext_s[pl.ds(SUBLANES - 1, t), :]
    u2 = uext_s[pl.ds(SUBLANES - 2, t), :]
    cw = cw_ref[...]
    oc = cb_ref[...] * (cw[0:1, :] * u2 + cw[1:2, :] * u1 + cw[2:3, :] * u)
    nst_ref[...] = uext_s[pl.ds(SUBLANES + t - (CONV_WIDTH - 1), CONV_WIDTH - 1), :]
    mix_ref[...] = jnp.concatenate([oa, ob, oc], axis=1)


def _sample_attention(layer, sinks, proj, caches, tables, cw, prev_outs):
    cak, cav, cbk, cbv, cst = caches
    n = cak.shape[1]
    t = proj[0].shape[0] // n
    proj3 = [a.reshape(n, t, a.shape[1]) for a in proj]
    per_seq = lambda a: pl.BlockSpec((None,) + a.shape[1:], lambda b: (b, 0, 0))
    per_layer_seq = lambda a: pl.BlockSpec((None, None) + a.shape[2:], lambda b: (layer, b, 0, 0))
    const = lambda a: pl.BlockSpec(a.shape, lambda b: (0, 0))
    taba, tabb = tables
    n_alias = 0 if prev_outs is None else len(prev_outs)
    in_specs = ([pl.BlockSpec(memory_space=pltpu.SMEM)] + [per_seq(a) for a in proj3]
                + [per_layer_seq(a) for a in caches] + [const(taba), const(tabb), const(cw)]
                + [pl.BlockSpec(memory_space=pl.ANY)] * n_alias)
    args = [sinks] + proj3 + list(caches) + [taba, tabb, cw] + (list(prev_outs) if n_alias else [])
    first_alias = len(args) - n_alias
    outs = pl.pallas_call(
        functools.partial(_sample_attn_kernel, n_alias),
        grid=(n,),
        in_specs=in_specs,
        out_specs=[pl.BlockSpec((None, t, 4 * C_WIDTH), lambda b: (b, 0, 0))]
                  + [per_layer_seq(a) for a in (cak, cav, cbk, cbv)]
                  + [pl.BlockSpec((None, CONV_WIDTH - 1, C_WIDTH), lambda b: (b, 0, 0))],
        out_shape=[jax.ShapeDtypeStruct((n, t, 4 * C_WIDTH), f32)]
                  + [jax.ShapeDtypeStruct(a.shape, f32) for a in (cak, cav, cbk, cbv)]
                  + [jax.ShapeDtypeStruct((n, CONV_WIDTH - 1, C_WIDTH), f32)],
        scratch_shapes=[pltpu.VMEM((SUBLANES + t, C_WIDTH), f32)],
        input_output_aliases={first_alias + i: 1 + i for i in range(n_alias)},
        compiler_params=pltpu.CompilerParams(dimension_semantics=("arbitrary",),
                                             vmem_limit_bytes=VMEM_LIMIT),
        name=f"sample_attn_l{layer}",
    )(*args)
    mix, nak, nav, nbk, nbv, nst = outs
    return mix.reshape(n * t, 4 * C_WIDTH), (nak, nav, nbk, nbv), nst


def _t5_bucket(dist):
    dist = jnp.maximum(dist, 0)
    max_exact = NUM_BUCKETS // 2
    scaled = jnp.log(jnp.maximum(dist, 1).astype(f32) / max_exact) / math.log(MAX_DISTANCE / max_exact)
    large = max_exact + (scaled * (NUM_BUCKETS - max_exact)).astype(jnp.int32)
    large = jnp.minimum(large, NUM_BUCKETS - 1)
    return jnp.where(dist < max_exact, dist, large)


def _band_bias(table, d):
    qi = jnp.arange(BLOCK)[:, None]
    kj = jnp.arange(2 * BLOCK)[None, :]
    delta = BLOCK + qi - kj
    in_band = (delta >= 0) & (delta <= BLOCK)
    bias = jnp.transpose(table[_t5_bucket(delta * d)].astype(f32), (2, 0, 1))
    with_prev = jnp.where(in_band[None], bias, NEG)
    no_prev = jnp.where((in_band & (kj >= BLOCK))[None], bias, NEG)
    return jnp.stack([no_prev, with_prev], axis=0)


def _sample_table_a(table, t, buf, cols):
    i = jnp.arange(t)[:, None]
    r = jnp.arange(cols)[None, :]
    delta = buf + i - r
    count = sum(((delta >= 0) & (delta % d == 0) & (delta // d <= w // d)).astype(f32) for w, d in A_PATTERNS)
    bias = jnp.transpose(table[_t5_bucket(delta)].astype(f32), (2, 0, 1))
    tab = jnp.where((count > 0)[None], bias + jnp.log(jnp.maximum(count, 1.0))[None], NEG)
    return tab.reshape(table.shape[1] * t, cols)


def _sample_table_b(table, t, buf, cols):
    i = jnp.arange(t)[:, None]
    r = jnp.arange(cols)[None, :]
    delta = buf + i - r
    valid = (delta >= 0) & (delta <= B_WINDOW)
    bias = jnp.transpose(table[_t5_bucket(delta)].astype(f32), (2, 0, 1))
    return jnp.where(valid[None], bias, NEG).reshape(table.shape[1] * t, cols)


def _expand_matrix():
    lane = jnp.arange(LANES)[:, None]
    col_head = jnp.arange(A_WIDTH)[None, :] // HEAD_DIM
    e = (lane == col_head).astype(bf16)
    return jnp.concatenate([e, e], axis=0)


def kernel(x_prompt, x_sample, cache_a_k, cache_a_v, cache_b_k, cache_b_v, state_conv, w_in, w_out, conv_w,
           b_sinks, rel_bias, g_mix_pre, g_mix_post, w_up, w_down, g_mlp_pre, g_mlp_post):
    depth = w_in.shape[0]
    batch, seq, d_model = x_prompt.shape
    n_dec, t_dec, _ = x_sample.shape
    buf_a, buf_b = cache_a_k.shape[2], cache_b_k.shape[2]
    assert batch == 1 and seq % (A_PATTERNS[-1][1] * BLOCK) == 0 and seq >= A_WINDOW
    assert buf_a == A_WINDOW and buf_b == B_WINDOW and t_dec == SUBLANES
    assert (n_dec * t_dec) % ROW_TILE == 0 and seq % ROW_TILE == 0

    xp = x_prompt.reshape(seq, d_model)
    xs = x_sample.reshape(n_dec * t_dec, d_model)
    to_fm = lambda c: jnp.transpose(c, (0, 1, 3, 4, 2)).reshape(depth, n_dec, c.shape[3] * HEAD_DIM, c.shape[2])
    from_fm = lambda c, heads: jnp.transpose(c.reshape(c.shape[:2] + (heads, HEAD_DIM, c.shape[3])), (0, 1, 4, 2, 3))
    caches = (to_fm(cache_a_k), to_fm(cache_a_v), to_fm(cache_b_k), to_fm(cache_b_v), state_conv)
    table_a, table_b = rel_bias[:, :HEADS], rel_bias[:, HEADS:]
    band_a = [_band_bias(table_a, d) for _, d in A_PATTERNS]
    band_b = _band_bias(table_b, 1)
    tables_s = (_sample_table_a(table_a, t_dec, buf_a, buf_a + LANES),
                _sample_table_b(table_b, t_dec, buf_b, buf_b + LANES))
    expand = _expand_matrix()
    row = lambda v: v.reshape(1, -1).astype(f32)

    new_p = [[] for _ in range(5)]
    conv_s = []
    new_s = None
    for l in range(depth):
        w_in_b, w_out_b = w_in[l].astype(bf16), w_out[l].astype(bf16)
        w_up_b, w_down_b = w_up[l].astype(bf16), w_down[l].astype(bf16)
        consts = [w_out_b, row(g_mix_post[l]), row(g_mlp_pre[l]), w_up_b, w_down_b, row(g_mlp_post[l])]
        cw = conv_w[l].astype(f32)
        sinks = b_sinks[l].astype(f32)

        qa, ka, va, qb, kbx, vbx, oc, kat, vat, kbt, vbt, ut = _prompt_proj(xp, row(g_mix_pre[l]), w_in_b, cw)
        pats = [_band_attention(qa, ka, va, band_a[i], d) for i, (_, d) in enumerate(A_PATTERNS)]
        ob = _band_attention(qb, kbx, vbx, band_b, 1, sinks=sinks)
        xp = _finish(_prompt_finish_kernel, "prompt_finish", xp,
                     [p[0] for p in pats] + [p[1] for p in pats] + [ob, oc], [expand] + consts)
        for lst, a in zip(new_p, (kat[None, None], vat[None, None], kbt[None, None], vbt[None, None],
                                  ut.reshape(1, CONV_WIDTH - 1, C_WIDTH))):
            lst.append(a)

        proj_s = _sample_proj(xs, row(g_mix_pre[l]), w_in_b)
        mix_s, new_s, nst = _sample_attention(l, sinks, proj_s, caches, tables_s, cw, new_s)
        conv_s.append(nst)
        xs = _finish(_sample_finish_kernel, "sample_finish", xs, [mix_s], consts)

    a_k_p, a_v_p, b_k_p, b_v_p = [jnp.concatenate(t, axis=0) for t in new_p[:4]]
    nak, nav, nbk, nbv = new_s
    return (xp.reshape(batch, seq, d_model), xs.reshape(n_dec, t_dec, d_model),
            from_fm(a_k_p, HEADS), from_fm(a_v_p, HEADS), from_fm(b_k_p, 2), from_fm(b_v_p, 2),
            jnp.stack(new_p[4], axis=0),
            from_fm(nak, HEADS), from_fm(nav, HEADS), from_fm(nbk, 2), from_fm(nbv, 2),
            jnp.stack(conv_s, axis=0))
```

```python
import functools
import math

import jax
import jax.numpy as jnp
from jax import lax
from jax.experimental import pallas as pl
from jax.experimental.pallas import tpu as pltpu

HEAD_DIM = 64
HEADS = 6
A_WIDTH = HEADS * HEAD_DIM
B_KV_WIDTH = 2 * HEAD_DIM
C_WIDTH = 256
A_PATTERNS = ((128, 1), (512, 4), (2048, 16))
A_WINDOW = 2048
B_WINDOW = 128
BLOCK = 128
CONV_WIDTH = 3
NUM_BUCKETS = 32
MAX_DISTANCE = 2048
EPS = 1e-6
SCALE = 1.0 / math.sqrt(HEAD_DIM)
NEG = -1e30
LANES = 128
SUBLANES = 8
ROW_TILE = 512
Q_TILE = 512
GROUP = 4
LOG2E = 1.4426950408889634
LN2 = 0.6931471805599453
FF_CHUNK = 1024
VMEM_LIMIT = 56 * 1024 * 1024

_IN_SIZES = (A_WIDTH, A_WIDTH, A_WIDTH, A_WIDTH, B_KV_WIDTH, B_KV_WIDTH, C_WIDTH, C_WIDTH, C_WIDTH)
_IN_OFFS = tuple(sum(_IN_SIZES[:i]) for i in range(len(_IN_SIZES) + 1))

f32 = jnp.float32
bf16 = jnp.bfloat16


def _rms(x, g):
    ms = jnp.mean(x * x, axis=-1, keepdims=True)
    return (x * lax.rsqrt(ms + EPS)) * g


def _dot(a, b):
    return jnp.dot(a, b, preferred_element_type=f32)


def _dot_nt(a, b):
    return lax.dot_general(a, b, (((1,), (1,)), ((), ())), preferred_element_type=f32)


def _project(x_ref, g_ref, w_ref):
    hb = _rms(x_ref[...], g_ref[...]).astype(bf16)
    return [_dot(hb, w_ref[:, _IN_OFFS[i]:_IN_OFFS[i + 1]]) for i in range(len(_IN_SIZES))]


def _swap_halves(t):
    return pltpu.roll(t, HEAD_DIM, axis=1)


def _low_half(shape):
    return lax.broadcasted_iota(jnp.int32, shape, 1) < HEAD_DIM


def _write_grouped(x, nat_ref, g4_ref, g16_ref, nat_s, g4_s):
    t, w = x.shape
    slabs = w // LANES
    nat_ref[...] = x.astype(bf16)
    for s in range(slabs):
        nat_s[s] = x[:, s * LANES:(s + 1) * LANES]
    for r in range(GROUP):
        for s in range(slabs):
            g = nat_s[s, pl.ds(r, t // GROUP, stride=GROUP), :]
            g4_s[r * slabs + s] = g
            g4_ref[:, (r * slabs + s) * LANES:(r * slabs + s + 1) * LANES] = g.astype(bf16)
    for c in range(GROUP * GROUP):
        r, r2 = c % GROUP, c // GROUP
        for s in range(slabs):
            h = g4_s[r * slabs + s, pl.ds(r2, t // (GROUP * GROUP), stride=GROUP), :]
            g16_ref[:, (c * slabs + s) * LANES:(c * slabs + s + 1) * LANES] = h.astype(bf16)


def _prompt_proj_kernel(n_steps, x_ref, g_ref, w_ref, cw_ref,
                        qa1_ref, qa4_ref, qa16_ref, ka1_ref, ka4_ref, ka16_ref, va1_ref, va4_ref, va16_ref,
                        qb_ref, kbx_ref, vbx_ref, oc_ref,
                        kat_ref, vat_ref, kbt_ref, vbt_ref, ut_ref, uext_ref, nat_s, g4_s):
    t = x_ref.shape[0]
    step = pl.program_id(0)
    aq, ak, av, bq, bk, bv, cb, cc, cx = _project(x_ref, g_ref, w_ref)
    _write_grouped(aq * (SCALE * LOG2E), qa1_ref, qa4_ref, qa16_ref, nat_s, g4_s)
    _write_grouped(ak, ka1_ref, ka4_ref, ka16_ref, nat_s, g4_s)
    _write_grouped(av, va1_ref, va4_ref, va16_ref, nat_s, g4_s)
    qb_ref[...] = (bq * (SCALE * LOG2E)).astype(bf16)
    low = _low_half(bk.shape)

    def widen(kv):
        sw = _swap_halves(kv)
        return jnp.concatenate([jnp.where(low, kv, sw), kv, jnp.where(low, sw, kv)], axis=1)

    kbx_ref[...] = widen(bk).astype(bf16)
    vbx_ref[...] = widen(bv).astype(bf16)

    @pl.when(step >= n_steps - A_WINDOW // t)
    def _():
        kat_ref[...] = ak.T
        vat_ref[...] = av.T

    @pl.when(step == 0)
    def _():
        uext_ref[0:SUBLANES, :] = jnp.zeros((SUBLANES, C_WIDTH), f32)

    u = cc * cx
    uext_ref[SUBLANES:, :] = u
    u1 = uext_ref[pl.ds(SUBLANES - 1, t), :]
    u2 = uext_ref[pl.ds(SUBLANES - 2, t), :]
    cw = cw_ref[...]
    conv = cw[0:1, :] * u2 + cw[1:2, :] * u1 + cw[2:3, :] * u
    oc_ref[...] = (cb * conv).astype(bf16)

    @pl.when(step == n_steps - 1)
    def _():
        kbt_ref[...] = bk[t - B_WINDOW:, :].T
        vbt_ref[...] = bv[t - B_WINDOW:, :].T
        ut_ref[...] = uext_ref[pl.ds(t + SUBLANES - (CONV_WIDTH - 1), CONV_WIDTH - 1), :]

    uext_ref[0:SUBLANES, :] = u[t - SUBLANES:, :]


def _prompt_proj(x, g, w, cw):
    s, d = x.shape
    t = ROW_TILE
    n = s // t
    tail_blocks = A_WINDOW // t
    row = lambda width: pl.BlockSpec((t, width), lambda i: (i, 0))
    const = lambda shape: pl.BlockSpec(shape, lambda i: (0, 0))
    tail = pl.BlockSpec((A_WIDTH, t), lambda i: (0, jnp.maximum(i - (n - tail_blocks), 0)))
    grouped_specs, grouped_shapes = [], []
    for _ in range(3):
        for _, dil in A_PATTERNS:
            grouped_specs.append(pl.BlockSpec((t // dil, dil * A_WIDTH), lambda i: (i, 0)))
            grouped_shapes.append(jax.ShapeDtypeStruct((s // dil, dil * A_WIDTH), bf16))
    bfo = lambda width: jax.ShapeDtypeStruct((s, width), bf16)
    return pl.pallas_call(
        functools.partial(_prompt_proj_kernel, n),
        grid=(n,),
        in_specs=[row(d), const((1, d)), const(w.shape), const(cw.shape)],
        out_specs=grouped_specs + [row(A_WIDTH)] * 3 + [row(C_WIDTH), tail, tail,
                                                        const((B_KV_WIDTH, B_WINDOW)), const((B_KV_WIDTH, B_WINDOW)),
                                                        const((CONV_WIDTH - 1, C_WIDTH))],
        out_shape=grouped_shapes + [bfo(A_WIDTH)] * 3 + [bfo(C_WIDTH),
                                                         jax.ShapeDtypeStruct((A_WIDTH, A_WINDOW), f32),
                                                         jax.ShapeDtypeStruct((A_WIDTH, A_WINDOW), f32),
                                                         jax.ShapeDtypeStruct((B_KV_WIDTH, B_WINDOW), f32),
                                                         jax.ShapeDtypeStruct((B_KV_WIDTH, B_WINDOW), f32),
                                                         jax.ShapeDtypeStruct((CONV_WIDTH - 1, C_WIDTH), f32)],
        scratch_shapes=[pltpu.VMEM((t + SUBLANES, C_WIDTH), f32),
                        pltpu.VMEM((A_WIDTH // LANES, t, LANES), f32),
                        pltpu.VMEM((GROUP * A_WIDTH // LANES, t // GROUP, LANES), f32)],
        compiler_params=pltpu.CompilerParams(dimension_semantics=("arbitrary",),
                                             vmem_limit_bytes=VMEM_LIMIT),
        name="prompt_proj",
    )(x, g, w, cw)


def _band_attn_kernel(gated, *refs):
    if gated:
        sink_ref, q_ref, kp_ref, kc_ref, vp_ref, vc_ref, bias_ref, o_ref = refs
    else:
        q_ref, kp_ref, kc_ref, vp_ref, vc_ref, bias_ref, o_ref, lse_ref = refs
    has_prev = jnp.minimum(pl.program_id(1), 1)
    kcat = jnp.concatenate([kp_ref[...], kc_ref[...]], axis=0)
    vcat = jnp.concatenate([vp_ref[...], vc_ref[...]], axis=0)
    low = _low_half((BLOCK, LANES))
    lane = lax.broadcasted_iota(jnp.int32, (BLOCK, LANES), 1)
    zero = jnp.zeros((BLOCK, LANES), bf16)
    for b in range(q_ref.shape[0] // BLOCK):
        rows = slice(b * BLOCK, (b + 1) * BLOCK)
        variant = has_prev if b == 0 else 1
        lse_tile = jnp.zeros((BLOCK, LANES), f32)
        for p in range(HEADS // 2):
            cols = slice(p * LANES, (p + 1) * LANES)
            qt = q_ref[rows, cols]
            kt = kcat[b * BLOCK:(b + 2) * BLOCK, cols]
            vt = vcat[b * BLOCK:(b + 2) * BLOCK, cols]
            halves = []
            for e in range(2):
                h = 2 * p + e
                qm = jnp.where(low if e == 0 else ~low, qt, zero)
                s = _dot_nt(qm, kt) + bias_ref[variant, h]
                m = jnp.max(s, axis=-1, keepdims=True)
                pe = jnp.exp2(s - m)
                l = jnp.sum(pe, axis=-1, keepdims=True)
                o = _dot(pe.astype(bf16), vt) / l
                lse = (m + jnp.log2(l)) * LN2
                if gated:
                    o = o * jax.nn.sigmoid(lse - sink_ref[h])
                else:
                    lse_tile = jnp.where(lane == h, lse, lse_tile)
                halves.append(o)
            o_ref[rows, cols] = jnp.where(low, halves[0], halves[1]).astype(bf16)
        if not gated:
            lse_ref[rows, :] = lse_tile


def _band_attention(q, k, v, bias, sinks=None):
    rows, width = q.shape
    d = width // A_WIDTH
    per_tile = Q_TILE // BLOCK
    cur = pl.BlockSpec((Q_TILE, A_WIDTH), lambda c, j: (j, c))
    prev = pl.BlockSpec((BLOCK, A_WIDTH), lambda c, j: (jnp.maximum(j * per_tile - 1, 0), c))
    bias_spec = pl.BlockSpec(bias.shape, lambda c, j: (0, 0, 0, 0))
    gated = sinks is not None
    in_specs = [cur, prev, cur, prev, cur, bias_spec]
    args = [q, k, k, v, v, bias]
    out_specs = [cur]
    out_shape = [jax.ShapeDtypeStruct((rows, width), bf16)]
    if gated:
        in_specs = [pl.BlockSpec(memory_space=pltpu.SMEM)] + in_specs
        args = [sinks] + args
    else:
        out_specs.append(pl.BlockSpec((Q_TILE, LANES), lambda c, j: (j, c)))
        out_shape.append(jax.ShapeDtypeStruct((rows, d * LANES), f32))
    outs = pl.pallas_call(
        functools.partial(_band_attn_kernel, gated),
        grid=(d, rows // Q_TILE),
        in_specs=in_specs, out_specs=out_specs, out_shape=out_shape,
        compiler_params=pltpu.CompilerParams(dimension_semantics=("arbitrary", "arbitrary"),
                                             vmem_limit_bytes=VMEM_LIMIT),
        name=f"band_attn_d{d}" + ("_gated" if gated else ""),
    )(*args)
    return outs[0] if gated else tuple(outs)


def _finish_rows(x, mix_b, wo_ref, gpost_ref, gpre2_ref, wup_ref, wdn_ref, gpost2_ref, out_ref):
    x1 = x + _rms(_dot(mix_b, wo_ref[...]), gpost_ref[...])
    hb = _rms(x1, gpre2_ref[...]).astype(bf16)
    ff = wup_ref.shape[1]
    acc = jnp.zeros(x.shape, f32)
    for c in range(ff // FF_CHUNK):
        cols = slice(c * FF_CHUNK, (c + 1) * FF_CHUNK)
        a = jnp.maximum(_dot(hb, wup_ref[:, cols]), 0.0)
        acc = acc + _dot((a * a).astype(bf16), wdn_ref[cols, :])
    out_ref[...] = x1 + _rms(acc, gpost2_ref[...])


def _ungroup(blk_ref, w, nat_s, g4_s):
    rows, width = blk_ref.shape
    slabs, d = w // LANES, width // w
    if d == 1:
        return blk_ref[...].astype(f32)
    if d == GROUP * GROUP:
        for c in range(d):
            r, r2 = c % GROUP, c // GROUP
            for s in range(slabs):
                col = (c * slabs + s) * LANES
                g4_s[r * slabs + s, pl.ds(r2, rows, stride=GROUP), :] = blk_ref[:, col:col + LANES].astype(f32)
        for r in range(GROUP):
            for s in range(slabs):
                nat_s[s, pl.ds(r, rows * GROUP, stride=GROUP), :] = g4_s[r * slabs + s]
    else:
        assert d == GROUP
        for r in range(GROUP):
            for s in range(slabs):
                col = (r * slabs + s) * LANES
                nat_s[s, pl.ds(r, rows, stride=GROUP), :] = blk_ref[:, col:col + LANES].astype(f32)
    return jnp.concatenate([nat_s[s] for s in range(slabs)], axis=1)


def _prompt_finish_kernel(x_ref, o1_ref, o4_ref, o16_ref, l1_ref, l4_ref, l16_ref, ob_ref, oc_ref, exp_ref,
                          wo_ref, gpost_ref, gpre2_ref, wup_ref, wdn_ref, gpost2_ref, out_ref, nat_s, g4_s):
    lses = [_ungroup(l_ref, LANES, nat_s, g4_s) for l_ref in (l1_ref, l4_ref, l16_ref)]
    m = jnp.maximum(jnp.maximum(lses[0], lses[1]), lses[2])
    es = [jnp.exp(l - m) for l in lses]
    tot = es[0] + es[1] + es[2]
    oa = jnp.zeros((x_ref.shape[0], A_WIDTH), f32)
    for e, o_ref in zip(es, (o1_ref, o4_ref, o16_ref)):
        w = e / tot
        w_hi = w.astype(bf16)
        w_lo = (w - w_hi.astype(f32)).astype(bf16)
        wx = _dot(jnp.concatenate([w_hi, w_lo], axis=1), exp_ref[...])
        oa = oa + wx * _ungroup(o_ref, A_WIDTH, nat_s, g4_s)
    mix_b = jnp.concatenate([oa.astype(bf16), ob_ref[...], oc_ref[...]], axis=1)
    _finish_rows(x_ref[...], mix_b, wo_ref, gpost_ref, gpre2_ref, wup_ref, wdn_ref, gpost2_ref, out_ref)


def _sample_finish_kernel(x_ref, mix_ref, wo_ref, gpost_ref, gpre2_ref, wup_ref, wdn_ref, gpost2_ref, out_ref):
    _finish_rows(x_ref[...], mix_ref[...].astype(bf16), wo_ref, gpost_ref, gpre2_ref, wup_ref, wdn_ref,
                 gpost2_ref, out_ref)


def _finish(kernel_fn, name, x, row_inputs, const_inputs, scratch_shapes=()):
    s, d = x.shape
    t = ROW_TILE
    row = lambda a: pl.BlockSpec((a.shape[0] * t // s, a.shape[1]), lambda i: (i, 0))
    const = lambda a: pl.BlockSpec(a.shape, lambda i: (0, 0), pipeline_mode=pl.Buffered(1))
    return pl.pallas_call(
        kernel_fn,
        grid=(s // t,),
        in_specs=[row(a) for a in [x] + row_inputs] + [const(a) for a in const_inputs],
        out_specs=pl.BlockSpec((t, d), lambda i: (i, 0)),
        out_shape=jax.ShapeDtypeStruct((s, d), f32),
        scratch_shapes=list(scratch_shapes),
        compiler_params=pltpu.CompilerParams(dimension_semantics=("arbitrary",),
                                             vmem_limit_bytes=VMEM_LIMIT),
        name=name,
    )(x, *row_inputs, *const_inputs)


def _sample_proj_kernel(x_ref, g_ref, w_ref, qa_ref, ka_ref, va_ref, qb_ref, kb_ref, vb_ref, cb_ref, u_ref):
    aq, ak, av, bq, bk, bv, cb, cc, cx = _project(x_ref, g_ref, w_ref)
    qa_ref[...] = aq * SCALE
    ka_ref[...] = ak
    va_ref[...] = av
    qb_ref[...] = bq * SCALE
    kb_ref[...] = bk
    vb_ref[...] = bv
    cb_ref[...] = cb
    u_ref[...] = cc * cx


def _sample_proj(x, g, w):
    s, d = x.shape
    t = ROW_TILE
    widths = (A_WIDTH, A_WIDTH, A_WIDTH, A_WIDTH, B_KV_WIDTH, B_KV_WIDTH, C_WIDTH, C_WIDTH)
    row = lambda width: pl.BlockSpec((t, width), lambda i: (i, 0))
    const = lambda shape: pl.BlockSpec(shape, lambda i: (0, 0))
    return pl.pallas_call(
        _sample_proj_kernel,
        grid=(s // t,),
        in_specs=[row(d), const((1, d)), const(w.shape)],
        out_specs=[row(wd) for wd in widths],
        out_shape=[jax.ShapeDtypeStruct((s, wd), f32) for wd in widths],
        compiler_params=pltpu.CompilerParams(dimension_semantics=("arbitrary",),
                                             vmem_limit_bytes=VMEM_LIMIT),
        name="sample_proj",
    )(x, g, w)


def _softmax_rows(s):
    m = jnp.max(s, axis=-1, keepdims=True)
    p = jnp.exp(s - m)
    l = jnp.sum(p, axis=-1, keepdims=True)
    return p, m, l


def _sample_attn_kernel(n_alias, sink_ref, qa_ref, ka_ref, va_ref, qb_ref, kb_ref, vb_ref, cb_ref, u_ref,
                        cak_ref, cav_ref, cbk_ref, cbv_ref, cst_ref, taba_ref, tabb_ref, cw_ref, *rest):
    mix_ref, nak_ref, nav_ref, nbk_ref, nbv_ref, nst_ref, uext_s = rest[n_alias:]
    t = qa_ref.shape[0]
    rows = HEADS * t
    shift_t = t.bit_length() - 1

    def pad_rows(new, at_end):
        z = jnp.zeros((LANES - t, new.shape[1]), f32)
        return jnp.concatenate([z, new] if at_end else [new, z], axis=0)

    def attend(qbd, kt_ref, vt_ref, k_new, v_new, tab_ref):
        buf = kt_ref.shape[1]
        s = jnp.concatenate([_dot(qbd, kt_ref[...].astype(bf16)),
                             _dot_nt(qbd, pad_rows(k_new, False).astype(bf16))], axis=1) + tab_ref[...]
        p, m, l = _softmax_rows(s)
        pb = p.astype(bf16)
        res = _dot_nt(pb[:, :buf], vt_ref[...].astype(bf16)) + _dot(pb[:, buf:], pad_rows(v_new, False).astype(bf16))
        return res / l, m + jnp.log(l)

    def shift(dst_ref, ct_ref, new):
        w, buf = ct_ref.shape
        new_t = jnp.transpose(pad_rows(new, True))
        rolled = pltpu.roll(ct_ref[...], buf - t, axis=1)
        if buf > LANES:
            dst_ref[:, 0:buf - LANES] = rolled[:, 0:buf - LANES]
        lane = lax.broadcasted_iota(jnp.int32, (w, LANES), 1)
        dst_ref[:, buf - LANES:] = jnp.where(lane >= LANES - t, new_t, rolled[:, buf - LANES:])

    ka_new, va_new = ka_ref[...], va_ref[...]
    q6 = jnp.concatenate([qa_ref[...]] * HEADS, axis=0)
    row_head = lax.broadcasted_iota(jnp.int32, (rows, A_WIDTH), 0) >> shift_t
    col_head = lax.broadcasted_iota(jnp.int32, (rows, A_WIDTH), 1) >> (HEAD_DIM.bit_length() - 1)
    own = row_head == col_head
    qbd = jnp.where(own, q6, 0.0).astype(bf16)
    res, _ = attend(qbd, cak_ref, cav_ref, ka_new, va_new, taba_ref)
    res = jnp.where(own, res, 0.0)
    oa = res[0:t, :]
    for h in range(1, HEADS):
        oa = oa + res[h * t:(h + 1) * t, :]
    shift(nak_ref, cak_ref, ka_new)
    shift(nav_ref, cav_ref, va_new)

    kb_new, vb_new = kb_ref[...], vb_ref[...]
    qb = qb_ref[...]
    low = _low_half((t, LANES))
    zero = jnp.zeros((t, LANES), f32)
    t0, t1, t2 = (qb[:, i * LANES:(i + 1) * LANES] for i in range(3))
    qbd_b = jnp.concatenate([
        jnp.where(low, t0, zero), jnp.where(low, _swap_halves(t0), zero), jnp.where(low, t1, zero),
        jnp.where(low, zero, t1), jnp.where(low, zero, _swap_halves(t2)), jnp.where(low, zero, t2)],
        axis=0).astype(bf16)
    resb, lse_b = attend(qbd_b, cbk_ref, cbv_ref, kb_new, vb_new, tabb_ref)
    row_head_b = lax.broadcasted_iota(jnp.int32, (rows, 1), 0) >> shift_t
    sink_col = jnp.zeros((rows, 1), f32)
    for h in range(HEADS):
        sink_col = jnp.where(row_head_b == h, sink_ref[h], sink_col)
    resb = resb * jax.nn.sigmoid(lse_b - sink_col)
    r = [resb[h * t:(h + 1) * t, :] for h in range(HEADS)]
    ob = jnp.concatenate([jnp.where(low, r[0], _swap_halves(r[1])), jnp.where(low, r[2], r[3]),
                          jnp.where(low, _swap_halves(r[4]), r[5])], axis=1)
    shift(nbk_ref, cbk_ref, kb_new)
    shift(nbv_ref, cbv_ref, vb_new)

    u = u_ref[...]
    uext_s[SUBLANES - (CONV_WIDTH - 1):SUBLANES, :] = cst_ref[...]
    uext_s[SUBLANES:, :] = u
    u1 = uext_s[pl.ds(SUBLANES - 1, t), :]
    u2 = uext_s[pl.ds(SUBLANES - 2, t), :]
    cw = cw_ref[...]
    oc = cb_ref[...] * (cw[0:1, :] * u2 + cw[1:2, :] * u1 + cw[2:3, :] * u)
    nst_ref[...] = uext_s[pl.ds(SUBLANES + t - (CONV_WIDTH - 1), CONV_WIDTH - 1), :]
    mix_ref[...] = jnp.concatenate([oa, ob, oc], axis=1)


def _sample_attention(layer, sinks, proj, caches, tables, cw, prev_outs):
    cak, cav, cbk, cbv, cst = caches
    n = cak.shape[1]
    t = proj[0].shape[0] // n
    proj3 = [a.reshape(n, t, a.shape[1]) for a in proj]
    per_seq = lambda a: pl.BlockSpec((None,) + a.shape[1:], lambda b: (b, 0, 0))
    per_layer_seq = lambda a: pl.BlockSpec((None, None) + a.shape[2:], lambda b: (layer, b, 0, 0))
    const = lambda a: pl.BlockSpec(a.shape, lambda b: (0, 0))
    taba, tabb = tables
    n_alias = 0 if prev_outs is None else len(prev_outs)
    in_specs = ([pl.BlockSpec(memory_space=pltpu.SMEM)] + [per_seq(a) for a in proj3]
                + [per_layer_seq(a) for a in caches] + [const(taba), const(tabb), const(cw)]
                + [pl.BlockSpec(memory_space=pl.ANY)] * n_alias)
    args = [sinks] + proj3 + list(caches) + [taba, tabb, cw] + (list(prev_outs) if n_alias else [])
    first_alias = len(args) - n_alias
    outs = pl.pallas_call(
        functools.partial(_sample_attn_kernel, n_alias),
        grid=(n,),
        in_specs=in_specs,
        out_specs=[pl.BlockSpec((None, t, 4 * C_WIDTH), lambda b: (b, 0, 0))]
                  + [per_layer_seq(a) for a in (cak, cav, cbk, cbv)]
                  + [pl.BlockSpec((None, CONV_WIDTH - 1, C_WIDTH), lambda b: (b, 0, 0))],
        out_shape=[jax.ShapeDtypeStruct((n, t, 4 * C_WIDTH), f32)]
                  + [jax.ShapeDtypeStruct(a.shape, f32) for a in (cak, cav, cbk, cbv)]
                  + [jax.ShapeDtypeStruct((n, CONV_WIDTH - 1, C_WIDTH), f32)],
        scratch_shapes=[pltpu.VMEM((SUBLANES + t, C_WIDTH), f32)],
        input_output_aliases={first_alias + i: 1 + i for i in range(n_alias)},
        compiler_params=pltpu.CompilerParams(dimension_semantics=("arbitrary",),
                                             vmem_limit_bytes=VMEM_LIMIT),
        name=f"sample_attn_l{layer}",
    )(*args)
    mix, nak, nav, nbk, nbv, nst = outs
    return mix.reshape(n * t, 4 * C_WIDTH), (nak, nav, nbk, nbv), nst


def _t5_bucket(dist):
    dist = jnp.maximum(dist, 0)
    max_exact = NUM_BUCKETS // 2
    scaled = jnp.log(jnp.maximum(dist, 1).astype(f32) / max_exact) / math.log(MAX_DISTANCE / max_exact)
    large = max_exact + (scaled * (NUM_BUCKETS - max_exact)).astype(jnp.int32)
    large = jnp.minimum(large, NUM_BUCKETS - 1)
    return jnp.where(dist < max_exact, dist, large)


def _bias_of(table, dist):
    onehot = jax.nn.one_hot(_t5_bucket(dist), NUM_BUCKETS, dtype=f32)
    return jnp.einsum('...b,bh->h...', onehot, table.astype(f32), precision=lax.Precision.HIGHEST)


def _band_bias(table, d):
    qi = jnp.arange(BLOCK)[:, None]
    kj = jnp.arange(2 * BLOCK)[None, :]
    delta = BLOCK + qi - kj
    in_band = (delta >= 0) & (delta <= BLOCK)
    bias = _bias_of(table, delta * d) * LOG2E
    with_prev = jnp.where(in_band[None], bias, NEG)
    no_prev = jnp.where((in_band & (kj >= BLOCK))[None], bias, NEG)
    return jnp.stack([no_prev, with_prev], axis=0)


def _sample_table_a(table, t, buf, cols):
    i = jnp.arange(t)[:, None]
    r = jnp.arange(cols)[None, :]
    delta = buf + i - r
    count = sum(((delta >= 0) & (delta % d == 0) & (delta // d <= w // d)).astype(f32) for w, d in A_PATTERNS)
    tab = jnp.where((count > 0)[None], _bias_of(table, delta) + jnp.log(jnp.maximum(count, 1.0))[None], NEG)
    return tab.reshape(table.shape[1] * t, cols)


def _sample_table_b(table, t, buf, cols):
    i = jnp.arange(t)[:, None]
    r = jnp.arange(cols)[None, :]
    delta = buf + i - r
    valid = (delta >= 0) & (delta <= B_WINDOW)
    return jnp.where(valid[None], _bias_of(table, delta), NEG).reshape(table.shape[1] * t, cols)


def _expand_matrix():
    lane = jnp.arange(LANES)[:, None]
    col_head = jnp.arange(A_WIDTH)[None, :] // HEAD_DIM
    e = (lane == col_head).astype(bf16)
    return jnp.concatenate([e, e], axis=0)


def kernel(x_prompt, x_sample, cache_a_k, cache_a_v, cache_b_k, cache_b_v, state_conv, w_in, w_out, conv_w,
           b_sinks, rel_bias, g_mix_pre, g_mix_post, w_up, w_down, g_mlp_pre, g_mlp_post):
    depth = w_in.shape[0]
    batch, seq, d_model = x_prompt.shape
    n_dec, t_dec, _ = x_sample.shape
    buf_a, buf_b = cache_a_k.shape[2], cache_b_k.shape[2]
    assert batch == 1 and seq % (A_PATTERNS[-1][1] * BLOCK) == 0 and seq >= A_WINDOW
    assert buf_a == A_WINDOW and buf_b == B_WINDOW and t_dec == SUBLANES
    assert (n_dec * t_dec) % ROW_TILE == 0 and seq % ROW_TILE == 0
    assert tuple(d for _, d in A_PATTERNS) == (1, GROUP, GROUP * GROUP) and seq % (GROUP * GROUP * Q_TILE) == 0

    xp = x_prompt.reshape(seq, d_model)
    xs = x_sample.reshape(n_dec * t_dec, d_model)
    to_fm = lambda c: jnp.transpose(c, (0, 1, 3, 4, 2)).reshape(depth, n_dec, c.shape[3] * HEAD_DIM, c.shape[2])
    from_fm = lambda c, heads: jnp.transpose(c.reshape(c.shape[:2] + (heads, HEAD_DIM, c.shape[3])), (0, 1, 4, 2, 3))
    caches = (to_fm(cache_a_k), to_fm(cache_a_v), to_fm(cache_b_k), to_fm(cache_b_v), state_conv)
    table_a, table_b = rel_bias[:, :HEADS], rel_bias[:, HEADS:]
    band_a = [_band_bias(table_a, d) for _, d in A_PATTERNS]
    band_b = _band_bias(table_b, 1)
    tables_s = (_sample_table_a(table_a, t_dec, buf_a, buf_a + LANES),
                _sample_table_b(table_b, t_dec, buf_b, buf_b + LANES))
    expand = _expand_matrix()
    row = lambda v: v.reshape(1, -1).astype(f32)

    new_p = [[] for _ in range(5)]
    conv_s = []
    new_s = None
    for l in range(depth):
        w_in_b, w_out_b = w_in[l].astype(bf16), w_out[l].astype(bf16)
        w_up_b, w_down_b = w_up[l].astype(bf16), w_down[l].astype(bf16)
        consts = [w_out_b, row(g_mix_post[l]), row(g_mlp_pre[l]), w_up_b, w_down_b, row(g_mlp_post[l])]
        cw = conv_w[l].astype(f32)
        sinks = b_sinks[l].astype(f32)

        (qa1, qa4, qa16, ka1, ka4, ka16, va1, va4, va16, qb, kbx, vbx, oc,
         kat, vat, kbt, vbt, ut) = _prompt_proj(xp, row(g_mix_pre[l]), w_in_b, cw)
        pats = [_band_attention(q, k, v, bias) for q, k, v, bias in
                ((qa1, ka1, va1, band_a[0]), (qa4, ka4, va4, band_a[1]), (qa16, ka16, va16, band_a[2]))]
        ob = _band_attention(qb, kbx, vbx, band_b, sinks=sinks)
        xp = _finish(_prompt_finish_kernel, "prompt_finish", xp,
                     [p[0] for p in pats] + [p[1] for p in pats] + [ob, oc], [expand] + consts,
                     scratch_shapes=[pltpu.VMEM((A_WIDTH // LANES, ROW_TILE, LANES), f32),
                                     pltpu.VMEM((GROUP * A_WIDTH // LANES, ROW_TILE // GROUP, LANES), f32)])
        for lst, a in zip(new_p, (kat[None, None], vat[None, None], kbt[None, None], vbt[None, None],
                                  ut.reshape(1, CONV_WIDTH - 1, C_WIDTH))):
            lst.append(a)

        proj_s = _sample_proj(xs, row(g_mix_pre[l]), w_in_b)
        mix_s, new_s, nst = _sample_attention(l, sinks, proj_s, caches, tables_s, cw, new_s)
        conv_s.append(nst)
        xs = _finish(_sample_finish_kernel, "sample_finish", xs, [mix_s], consts)

    a_k_p, a_v_p, b_k_p, b_v_p = [jnp.concatenate(t, axis=0) for t in new_p[:4]]
    nak, nav, nbk, nbv = new_s
    return (xp.reshape(batch, seq, d_model), xs.reshape(n_dec, t_dec, d_model),
            from_fm(a_k_p, HEADS), from_fm(a_v_p, HEADS), from_fm(b_k_p, 2), from_fm(b_v_p, 2),
            jnp.stack(new_p[4], axis=0),
            from_fm(nak, HEADS), from_fm(nav, HEADS), from_fm(nbk, 2), from_fm(nbv, 2),
            jnp.stack(conv_s, axis=0))
```

```python
import functools
import math

import jax
import jax.numpy as jnp
from jax import lax
from jax.experimental import pallas as pl
from jax.experimental.pallas import tpu as pltpu

HEAD_DIM = 64
HEADS = 6
A_WIDTH = HEADS * HEAD_DIM
B_KV_WIDTH = 2 * HEAD_DIM
C_WIDTH = 256
A_PATTERNS = ((128, 1), (512, 4), (2048, 16))
A_WINDOW = 2048
B_WINDOW = 128
BLOCK = 128
CONV_WIDTH = 3
NUM_BUCKETS = 32
MAX_DISTANCE = 2048
EPS = 1e-6
SCALE = 1.0 / math.sqrt(HEAD_DIM)
NEG = -1e30
LANES = 128
SUBLANES = 8
ROW_TILE = 512
Q_TILE = 512
GROUP = 4
FF_CHUNK = 1024
SEQS_PER_STEP = 16
VMEM_LIMIT = 56 * 1024 * 1024
VMEM_LIMIT_STREAMING = 60 * 1024 * 1024

_IN_SIZES = (A_WIDTH, A_WIDTH, A_WIDTH, A_WIDTH, B_KV_WIDTH, B_KV_WIDTH, C_WIDTH, C_WIDTH, C_WIDTH)
_IN_OFFS = tuple(sum(_IN_SIZES[:i]) for i in range(len(_IN_SIZES) + 1))

f32 = jnp.float32
bf16 = jnp.bfloat16


def _rms(x, g):
    ms = jnp.mean(x * x, axis=-1, keepdims=True)
    return (x * lax.rsqrt(ms + EPS)) * g


def _dot(a, b):
    return jnp.dot(a, b, preferred_element_type=f32)


def _dot_nt(a, b):
    return lax.dot_general(a, b, (((1,), (1,)), ((), ())), preferred_element_type=f32)


def _project(x_ref, g_ref, w_ref):
    hb = _rms(x_ref[...], g_ref[...]).astype(bf16)
    proj = _dot(hb, w_ref[...])
    return [proj[:, _IN_OFFS[i]:_IN_OFFS[i + 1]] for i in range(len(_IN_SIZES))]


def _swap_halves(t):
    return pltpu.roll(t, HEAD_DIM, axis=1)


def _low_half(shape):
    return lax.broadcasted_iota(jnp.int32, shape, 1) < HEAD_DIM


def _layer_block(a, layer):
    return pl.BlockSpec((None,) + a.shape[1:], lambda *_: (layer, 0, 0), pipeline_mode=pl.Buffered(1))


def _write_grouped(x, nat_ref, g4_ref, g16_ref, nat_s, g4_s):
    t, w = x.shape
    slabs = w // LANES
    nat_ref[...] = x.astype(bf16)
    for s in range(slabs):
        nat_s[s] = x[:, s * LANES:(s + 1) * LANES]
    for r in range(GROUP):
        for s in range(slabs):
            g = nat_s[s, pl.ds(r, t // GROUP, stride=GROUP), :]
            g4_s[r * slabs + s] = g
            g4_ref[:, (r * slabs + s) * LANES:(r * slabs + s + 1) * LANES] = g.astype(bf16)
    for c in range(GROUP * GROUP):
        r, r2 = c % GROUP, c // GROUP
        for s in range(slabs):
            h = g4_s[r * slabs + s, pl.ds(r2, t // (GROUP * GROUP), stride=GROUP), :]
            g16_ref[:, (c * slabs + s) * LANES:(c * slabs + s + 1) * LANES] = h.astype(bf16)


def _prompt_proj_kernel(n_steps, x_ref, g_ref, w_ref, cw_ref,
                        qa1_ref, qa4_ref, qa16_ref, ka1_ref, ka4_ref, ka16_ref, va1_ref, va4_ref, va16_ref,
                        qb_ref, kbx_ref, vbx_ref, oc_ref,
                        kat_ref, vat_ref, kbt_ref, vbt_ref, ut_ref, uext_ref, nat_s, g4_s):
    t = x_ref.shape[0]
    step = pl.program_id(0)
    aq, ak, av, bq, bk, bv, cb, cc, cx = _project(x_ref, g_ref, w_ref)
    _write_grouped(aq * SCALE, qa1_ref, qa4_ref, qa16_ref, nat_s, g4_s)
    _write_grouped(ak, ka1_ref, ka4_ref, ka16_ref, nat_s, g4_s)
    _write_grouped(av, va1_ref, va4_ref, va16_ref, nat_s, g4_s)
    qb_ref[...] = (bq * SCALE).astype(bf16)
    low = _low_half(bk.shape)

    def widen(kv):
        sw = _swap_halves(kv)
        return jnp.concatenate([jnp.where(low, kv, sw), kv, jnp.where(low, sw, kv)], axis=1)

    kbx_ref[...] = widen(bk).astype(bf16)
    vbx_ref[...] = widen(bv).astype(bf16)

    @pl.when(step >= n_steps - A_WINDOW // t)
    def _():
        kat_ref[...] = ak.T
        vat_ref[...] = av.T

    @pl.when(step == 0)
    def _():
        uext_ref[0:SUBLANES, :] = jnp.zeros((SUBLANES, C_WIDTH), f32)

    u = cc * cx
    uext_ref[SUBLANES:, :] = u
    u1 = uext_ref[pl.ds(SUBLANES - 1, t), :]
    u2 = uext_ref[pl.ds(SUBLANES - 2, t), :]
    cw = cw_ref[...]
    conv = cw[0:1, :] * u2 + cw[1:2, :] * u1 + cw[2:3, :] * u
    oc_ref[...] = (cb * conv).astype(bf16)

    @pl.when(step == n_steps - 1)
    def _():
        kbt_ref[...] = bk[t - B_WINDOW:, :].T
        vbt_ref[...] = bv[t - B_WINDOW:, :].T
        ut_ref[...] = uext_ref[pl.ds(t + SUBLANES - (CONV_WIDTH - 1), CONV_WIDTH - 1), :]

    uext_ref[0:SUBLANES, :] = u[t - SUBLANES:, :]


def _prompt_proj(layer, x, g, w, cw):
    s, d = x.shape
    t = ROW_TILE
    n = s // t
    tail_blocks = A_WINDOW // t
    row = lambda width: pl.BlockSpec((t, width), lambda i: (i, 0))
    const = lambda shape: pl.BlockSpec(shape, lambda i: (0, 0))
    tail = pl.BlockSpec((A_WIDTH, t), lambda i: (0, jnp.maximum(i - (n - tail_blocks), 0)))
    grouped_specs, grouped_shapes = [], []
    for _ in range(3):
        for _, dil in A_PATTERNS:
            grouped_specs.append(pl.BlockSpec((t // dil, dil * A_WIDTH), lambda i: (i, 0)))
            grouped_shapes.append(jax.ShapeDtypeStruct((s // dil, dil * A_WIDTH), bf16))
    bfo = lambda width: jax.ShapeDtypeStruct((s, width), bf16)
    return pl.pallas_call(
        functools.partial(_prompt_proj_kernel, n),
        grid=(n,),
        in_specs=[row(d), const((1, d)), _layer_block(w, layer), const(cw.shape)],
        out_specs=grouped_specs + [row(A_WIDTH)] * 3 + [row(C_WIDTH), tail, tail,
                                                        const((B_KV_WIDTH, B_WINDOW)), const((B_KV_WIDTH, B_WINDOW)),
                                                        const((CONV_WIDTH - 1, C_WIDTH))],
        out_shape=grouped_shapes + [bfo(A_WIDTH)] * 3 + [bfo(C_WIDTH),
                                                         jax.ShapeDtypeStruct((A_WIDTH, A_WINDOW), f32),
                                                         jax.ShapeDtypeStruct((A_WIDTH, A_WINDOW), f32),
                                                         jax.ShapeDtypeStruct((B_KV_WIDTH, B_WINDOW), f32),
                                                         jax.ShapeDtypeStruct((B_KV_WIDTH, B_WINDOW), f32),
                                                         jax.ShapeDtypeStruct((CONV_WIDTH - 1, C_WIDTH), f32)],
        scratch_shapes=[pltpu.VMEM((t + SUBLANES, C_WIDTH), f32),
                        pltpu.VMEM((A_WIDTH // LANES, t, LANES), f32),
                        pltpu.VMEM((GROUP * A_WIDTH // LANES, t // GROUP, LANES), f32)],
        compiler_params=pltpu.CompilerParams(dimension_semantics=("arbitrary",),
                                             vmem_limit_bytes=VMEM_LIMIT),
        name="prompt_proj",
    )(x, g, w, cw)


def _band_attn_kernel(gated, *refs):
    if gated:
        sink_ref, q_ref, kp_ref, kc_ref, vp_ref, vc_ref, bias_ref, o_ref = refs
    else:
        q_ref, kp_ref, kc_ref, vp_ref, vc_ref, bias_ref, o_ref, lse_ref = refs
    has_prev = jnp.minimum(pl.program_id(1), 1)
    kcat = jnp.concatenate([kp_ref[...], kc_ref[...]], axis=0)
    vcat = jnp.concatenate([vp_ref[...], vc_ref[...]], axis=0)
    low = _low_half((BLOCK, LANES))
    lane = lax.broadcasted_iota(jnp.int32, (BLOCK, LANES), 1)
    zero = jnp.zeros((BLOCK, LANES), bf16)
    for b in range(q_ref.shape[0] // BLOCK):
        rows = slice(b * BLOCK, (b + 1) * BLOCK)
        variant = has_prev if b == 0 else 1
        lse_tile = jnp.zeros((BLOCK, LANES), f32)
        for p in range(HEADS // 2):
            cols = slice(p * LANES, (p + 1) * LANES)
            qt = q_ref[rows, cols]
            kt = kcat[b * BLOCK:(b + 2) * BLOCK, cols]
            vt = vcat[b * BLOCK:(b + 2) * BLOCK, cols]
            halves = []
            for e in range(2):
                h = 2 * p + e
                qm = jnp.where(low if e == 0 else ~low, qt, zero)
                s = _dot_nt(qm, kt) + bias_ref[variant, h]
                m = jnp.max(s, axis=-1, keepdims=True)
                pe = jnp.exp(s - m)
                l = jnp.sum(pe, axis=-1, keepdims=True)
                o = _dot(pe.astype(bf16), vt) / l
                lse = m + jnp.log(l)
                if gated:
                    o = o * jax.nn.sigmoid(lse - sink_ref[h])
                else:
                    lse_tile = jnp.where(lane == h, lse, lse_tile)
                halves.append(o)
            o_ref[rows, cols] = jnp.where(low, halves[0], halves[1]).astype(bf16)
        if not gated:
            lse_ref[rows, :] = lse_tile


def _band_attention(q, k, v, bias, sinks=None):
    rows, width = q.shape
    d = width // A_WIDTH
    per_tile = Q_TILE // BLOCK
    cur = pl.BlockSpec((Q_TILE, A_WIDTH), lambda c, j: (j, c))
    prev = pl.BlockSpec((BLOCK, A_WIDTH), lambda c, j: (jnp.maximum(j * per_tile - 1, 0), c))
    bias_spec = pl.BlockSpec(bias.shape, lambda c, j: (0, 0, 0, 0))
    gated = sinks is not None
    in_specs = [cur, prev, cur, prev, cur, bias_spec]
    args = [q, k, k, v, v, bias]
    out_specs = [cur]
    out_shape = [jax.ShapeDtypeStruct((rows, width), bf16)]
    if gated:
        in_specs = [pl.BlockSpec(memory_space=pltpu.SMEM)] + in_specs
        args = [sinks] + args
    else:
        out_specs.append(pl.BlockSpec((Q_TILE, LANES), lambda c, j: (j, c)))
        out_shape.append(jax.ShapeDtypeStruct((rows, d * LANES), f32))
    outs = pl.pallas_call(
        functools.partial(_band_attn_kernel, gated),
        grid=(d, rows // Q_TILE),
        in_specs=in_specs, out_specs=out_specs, out_shape=out_shape,
        compiler_params=pltpu.CompilerParams(dimension_semantics=("arbitrary", "arbitrary"),
                                             vmem_limit_bytes=VMEM_LIMIT),
        name=f"band_attn_d{d}" + ("_gated" if gated else ""),
    )(*args)
    return outs[0] if gated else tuple(outs)


def _pad_rows(new, at_end):
    z = jnp.zeros((LANES - new.shape[0], new.shape[1]), f32)
    return jnp.concatenate([z, new] if at_end else [new, z], axis=0)


def _softmax_rows(s):
    m = jnp.max(s, axis=-1, keepdims=True)
    p = jnp.exp(s - m)
    l = jnp.sum(p, axis=-1, keepdims=True)
    return p, m, l


def _cache_scores(qbd, kt_ref, k_new, tab_ref):
    s = jnp.concatenate([_dot(qbd, kt_ref[...].astype(bf16)),
                         _dot_nt(qbd, _pad_rows(k_new, False).astype(bf16))], axis=1) + tab_ref[...]
    p, m, l = _softmax_rows(s)
    return p.astype(bf16), m, l


def _cache_values(pb, l, vt_ref, v_new):
    buf = vt_ref.shape[1]
    res = _dot_nt(pb[:, :buf], vt_ref[...].astype(bf16)) + _dot(pb[:, buf:], _pad_rows(v_new, False).astype(bf16))
    return res / l


def _shift_into(dst_ref, ct_ref, new):
    w, buf = ct_ref.shape
    t = new.shape[0]
    new_t = jnp.transpose(_pad_rows(new, True))
    rolled = pltpu.roll(ct_ref[...], buf - t, axis=1)
    if buf > LANES:
        dst_ref[:, 0:buf - LANES] = rolled[:, 0:buf - LANES]
    lane = lax.broadcasted_iota(jnp.int32, (w, LANES), 1)
    dst_ref[:, buf - LANES:] = jnp.where(lane >= LANES - t, new_t, rolled[:, buf - LANES:])


def _own_head_mask(t):
    rows = HEADS * t
    row_head = lax.broadcasted_iota(jnp.int32, (rows, A_WIDTH), 0) >> (t.bit_length() - 1)
    col_head = lax.broadcasted_iota(jnp.int32, (rows, A_WIDTH), 1) >> (HEAD_DIM.bit_length() - 1)
    return row_head == col_head


def _out_proj_stage(x, mix_b, wo_ref, gpost_ref, gpre2_ref):
    x1 = x + _rms(_dot(mix_b, wo_ref[...]), gpost_ref[...])
    return x1, _rms(x1, gpre2_ref[...]).astype(bf16)


def _ffn_chunk_stage(hb, acc, wup_ref, wdn_ref, c):
    cols = slice(c * FF_CHUNK, (c + 1) * FF_CHUNK)
    a = jnp.maximum(_dot(hb, wup_ref[:, cols]), 0.0)
    return acc + _dot((a * a).astype(bf16), wdn_ref[cols, :])


def _ungroup(blk_ref, w, nat_s, g4_s):
    rows, width = blk_ref.shape
    slabs, d = w // LANES, width // w
    if d == 1:
        return blk_ref[...].astype(f32)
    if d == GROUP * GROUP:
        for c in range(d):
            r, r2 = c % GROUP, c // GROUP
            for s in range(slabs):
                col = (c * slabs + s) * LANES
                g4_s[r * slabs + s, pl.ds(r2, rows, stride=GROUP), :] = blk_ref[:, col:col + LANES].astype(f32)
        for r in range(GROUP):
            for s in range(slabs):
                nat_s[s, pl.ds(r, rows * GROUP, stride=GROUP), :] = g4_s[r * slabs + s]
    else:
        assert d == GROUP
        for r in range(GROUP):
            for s in range(slabs):
                col = (r * slabs + s) * LANES
                nat_s[s, pl.ds(r, rows, stride=GROUP), :] = blk_ref[:, col:col + LANES].astype(f32)
    return jnp.concatenate([nat_s[s] for s in range(slabs)], axis=1)


def _prompt_finish_kernel(layer, n_alias, x_ref, o1_ref, o4_ref, o16_ref, l1_ref, l4_ref, l16_ref, ob_ref, oc_ref,
                          qa_ref, ka_ref, va_ref, exp_ref, taba_ref, wo_ref, gpost_ref, gpre2_ref, wup_ref, wdn_ref,
                          gpost2_ref, cak_hbm, cav_hbm, *rest):
    out_ref, oa_ref, nak_hbm, nav_hbm, nat_s, g4_s, in_buf, out_buf, in_sem, out_sem = rest[n_alias:]
    step, n_steps = pl.program_id(0), pl.num_programs(0)
    nseq, t = qa_ref.shape[0], qa_ref.shape[1]
    n_units = 2 * nseq
    own = _own_head_mask(t)
    st = {}

    def read(u, base):
        src = (cak_hbm, cav_hbm)[u % 2].at[layer, base + u // 2]
        return pltpu.make_async_copy(src, in_buf.at[u % 2], in_sem.at[u % 2])

    def write(u, base):
        dst = (nak_hbm, nav_hbm)[u % 2].at[layer, base + u // 2]
        return pltpu.make_async_copy(out_buf.at[u % 2], dst, out_sem.at[u % 2])

    def merge_stage():
        lses = [_ungroup(l_ref, LANES, nat_s, g4_s) for l_ref in (l1_ref, l4_ref, l16_ref)]
        m = jnp.maximum(jnp.maximum(lses[0], lses[1]), lses[2])
        es = [jnp.exp(l - m) for l in lses]
        tot = es[0] + es[1] + es[2]
        oa = jnp.zeros((x_ref.shape[0], A_WIDTH), f32)
        for e, o_ref in zip(es, (o1_ref, o4_ref, o16_ref)):
            w = e / tot
            w_hi = w.astype(bf16)
            w_lo = (w - w_hi.astype(f32)).astype(bf16)
            wx = _dot(jnp.concatenate([w_hi, w_lo], axis=1), exp_ref[...])
            oa = oa + wx * _ungroup(o_ref, A_WIDTH, nat_s, g4_s)
        st["mix"] = jnp.concatenate([oa.astype(bf16), ob_ref[...], oc_ref[...]], axis=1)

    def proj_stage():
        st["x1"], st["hb"] = _out_proj_stage(x_ref[...], st["mix"], wo_ref, gpost_ref, gpre2_ref)
        st["acc"] = jnp.zeros(x_ref.shape, f32)

    def ffn_stage(c):
        st["acc"] = _ffn_chunk_stage(st["hb"], st["acc"], wup_ref, wdn_ref, c)

    def final_stage():
        out_ref[...] = st["x1"] + _rms(st["acc"], gpost2_ref[...])

    stages = ([merge_stage, proj_stage]
              + [functools.partial(ffn_stage, c) for c in range(wup_ref.shape[1] // FF_CHUNK)] + [final_stage])

    def keys_unit(s):
        q6 = jnp.concatenate([qa_ref[s]] * HEADS, axis=0)
        qbd = jnp.where(own, q6, 0.0).astype(bf16)
        st["p"], _, st["l"] = _cache_scores(qbd, in_buf.at[0], ka_ref[s], taba_ref)
        _shift_into(out_buf.at[0], in_buf.at[0], ka_ref[s])

    def values_unit(s):
        res = jnp.where(own, _cache_values(st["p"], st["l"], in_buf.at[1], va_ref[s]), 0.0)
        oa = res[0:t, :]
        for h in range(1, HEADS):
            oa = oa + res[h * t:(h + 1) * t, :]
        oa_ref[s] = oa
        _shift_into(out_buf.at[1], in_buf.at[1], va_ref[s])

    base = step * nseq

    @pl.when(step == 0)
    def _():
        read(0, base).start()

    for u in range(n_units):
        if u + 1 < n_units:
            read(u + 1, base).start()
        else:
            @pl.when(step + 1 < n_steps)
            def _():
                read(0, base + nseq).start()
        read(u, base).wait()
        if u >= 2:
            write(u - 2, base).wait()
        else:
            @pl.when(step > 0)
            def _():
                write(u, base).wait()
        if u < len(stages):
            stages[u]()
        (keys_unit if u % 2 == 0 else values_unit)(u // 2)
        write(u, base).start()
    for stage in stages[n_units:]:
        stage()

    @pl.when(step == n_steps - 1)
    def _():
        write(n_units - 2, base).wait()
        write(n_units - 1, base).wait()


def _prompt_finish(layer, x, row_inputs, sample_qkv, expand, taba, params, caches, prev_outs):
    s, d = x.shape
    t = ROW_TILE
    n_steps = s // t
    n_seq = sample_qkv[0].shape[0]
    assert n_seq % n_steps == 0
    per_step = n_seq // n_steps
    row = lambda a: pl.BlockSpec((a.shape[0] * t // s, a.shape[1]), lambda i: (i, 0))
    seqs = lambda a: pl.BlockSpec((per_step,) + a.shape[1:], lambda i: (i, 0, 0))
    const = lambda a: pl.BlockSpec(a.shape, lambda i: (0, 0), pipeline_mode=pl.Buffered(1))
    gain = lambda a: pl.BlockSpec((None, 1, a.shape[2]), lambda i: (layer, 0, 0))
    w_out, g_post, g_pre2, w_up, w_down, g_post2 = params
    cak, cav = caches
    n_alias = 0 if prev_outs is None else len(prev_outs)
    any_spec = pl.BlockSpec(memory_space=pl.ANY)
    args = ([x] + row_inputs + list(sample_qkv)
            + [expand, taba, w_out, g_post, g_pre2, w_up, w_down, g_post2, cak, cav]
            + (list(prev_outs) if n_alias else []))
    in_specs = ([row(a) for a in [x] + row_inputs] + [seqs(a) for a in sample_qkv]
                + [const(expand), const(taba), _layer_block(w_out, layer), gain(g_post), gain(g_pre2),
                   _layer_block(w_up, layer), _layer_block(w_down, layer), gain(g_post2), any_spec, any_spec]
                + [any_spec] * n_alias)
    first_alias = len(args) - n_alias
    buf_shape = (2,) + cak.shape[2:]
    outs = pl.pallas_call(
        functools.partial(_prompt_finish_kernel, layer, n_alias),
        grid=(n_steps,),
        in_specs=in_specs,
        out_specs=[pl.BlockSpec((t, d), lambda i: (i, 0)), seqs(sample_qkv[0]), any_spec, any_spec],
        out_shape=[jax.ShapeDtypeStruct((s, d), f32), jax.ShapeDtypeStruct(sample_qkv[0].shape, f32),
                   jax.ShapeDtypeStruct(cak.shape, f32), jax.ShapeDtypeStruct(cav.shape, f32)],
        scratch_shapes=[pltpu.VMEM((A_WIDTH // LANES, t, LANES), f32),
                        pltpu.VMEM((GROUP * A_WIDTH // LANES, t // GROUP, LANES), f32),
                        pltpu.VMEM(buf_shape, f32), pltpu.VMEM(buf_shape, f32),
                        pltpu.SemaphoreType.DMA((2,)), pltpu.SemaphoreType.DMA((2,))],
        input_output_aliases={first_alias + i: 2 + i for i in range(n_alias)},
        compiler_params=pltpu.CompilerParams(dimension_semantics=("arbitrary",),
                                             vmem_limit_bytes=VMEM_LIMIT_STREAMING),
        name=f"prompt_finish_l{layer}",
    )(*args)
    return outs[0], outs[1], (outs[2], outs[3])


def _sample_finish_kernel(x_ref, mix_ref, wo_ref, gpost_ref, gpre2_ref, wup_ref, wdn_ref, gpost2_ref, out_ref):
    x1, hb = _out_proj_stage(x_ref[...], mix_ref[...].astype(bf16), wo_ref, gpost_ref, gpre2_ref)
    acc = jnp.zeros(x1.shape, f32)
    for c in range(wup_ref.shape[1] // FF_CHUNK):
        acc = _ffn_chunk_stage(hb, acc, wup_ref, wdn_ref, c)
    out_ref[...] = x1 + _rms(acc, gpost2_ref[...])


def _sample_finish(layer, x, mix, params):
    s, d = x.shape
    t = ROW_TILE
    row = lambda a: pl.BlockSpec((t, a.shape[1]), lambda i: (i, 0))
    gain = lambda a: pl.BlockSpec((None, 1, a.shape[2]), lambda i: (layer, 0, 0))
    w_out, g_post, g_pre2, w_up, w_down, g_post2 = params
    return pl.pallas_call(
        _sample_finish_kernel,
        grid=(s // t,),
        in_specs=[row(x), row(mix), _layer_block(w_out, layer), gain(g_post), gain(g_pre2),
                  _layer_block(w_up, layer), _layer_block(w_down, layer), gain(g_post2)],
        out_specs=pl.BlockSpec((t, d), lambda i: (i, 0)),
        out_shape=jax.ShapeDtypeStruct((s, d), f32),
        compiler_params=pltpu.CompilerParams(dimension_semantics=("arbitrary",),
                                             vmem_limit_bytes=VMEM_LIMIT),
        name="sample_finish",
    )(x, mix, w_out, g_post, g_pre2, w_up, w_down, g_post2)


def _sample_proj_kernel(x_ref, g_ref, w_ref, qa_ref, ka_ref, va_ref, qb_ref, kb_ref, vb_ref, cb_ref, u_ref):
    aq, ak, av, bq, bk, bv, cb, cc, cx = _project(x_ref, g_ref, w_ref)
    qa_ref[...] = aq * SCALE
    ka_ref[...] = ak
    va_ref[...] = av
    qb_ref[...] = bq * SCALE
    kb_ref[...] = bk
    vb_ref[...] = bv
    cb_ref[...] = cb
    u_ref[...] = cc * cx


def _sample_proj(layer, x, g, w):
    s, d = x.shape
    t = ROW_TILE
    widths = (A_WIDTH, A_WIDTH, A_WIDTH, A_WIDTH, B_KV_WIDTH, B_KV_WIDTH, C_WIDTH, C_WIDTH)
    row = lambda width: pl.BlockSpec((t, width), lambda i: (i, 0))
    const = lambda shape: pl.BlockSpec(shape, lambda i: (0, 0))
    return pl.pallas_call(
        _sample_proj_kernel,
        grid=(s // t,),
        in_specs=[row(d), const((1, d)), _layer_block(w, layer)],
        out_specs=[row(wd) for wd in widths],
        out_shape=[jax.ShapeDtypeStruct((s, wd), f32) for wd in widths],
        compiler_params=pltpu.CompilerParams(dimension_semantics=("arbitrary",),
                                             vmem_limit_bytes=VMEM_LIMIT),
        name="sample_proj",
    )(x, g, w)


def _sample_rest_kernel(n_alias, sink_ref, oa_ref, qb_ref, kb_ref, vb_ref, cb_ref, u_ref,
                        cbk_ref, cbv_ref, cst_ref, tabb_ref, cw_ref, *rest):
    mix_ref, nbk_ref, nbv_ref, nst_ref, uext_s = rest[n_alias:]
    t = qb_ref.shape[1]
    rows = HEADS * t
    low = _low_half((t, LANES))
    zero = jnp.zeros((t, LANES), f32)
    row_head = lax.broadcasted_iota(jnp.int32, (rows, 1), 0) >> (t.bit_length() - 1)
    sink_col = jnp.zeros((rows, 1), f32)
    for h in range(HEADS):
        sink_col = jnp.where(row_head == h, sink_ref[h], sink_col)
    cw = cw_ref[...]
    for g in range(qb_ref.shape[0]):
        kb_new, vb_new = kb_ref[g], vb_ref[g]
        qb = qb_ref[g]
        t0, t1, t2 = (qb[:, i * LANES:(i + 1) * LANES] for i in range(3))
        qbd = jnp.concatenate([
            jnp.where(low, t0, zero), jnp.where(low, _swap_halves(t0), zero), jnp.where(low, t1, zero),
            jnp.where(low, zero, t1), jnp.where(low, zero, _swap_halves(t2)), jnp.where(low, zero, t2)],
            axis=0).astype(bf16)
        pb, mb, lb = _cache_scores(qbd, cbk_ref.at[g], kb_new, tabb_ref)
        resb = _cache_values(pb, lb, cbv_ref.at[g], vb_new) * jax.nn.sigmoid(mb + jnp.log(lb) - sink_col)
        r = [resb[h * t:(h + 1) * t, :] for h in range(HEADS)]
        ob = jnp.concatenate([jnp.where(low, r[0], _swap_halves(r[1])), jnp.where(low, r[2], r[3]),
                              jnp.where(low, _swap_halves(r[4]), r[5])], axis=1)
        _shift_into(nbk_ref.at[g], cbk_ref.at[g], kb_new)
        _shift_into(nbv_ref.at[g], cbv_ref.at[g], vb_new)

        u = u_ref[g]
        uext_s[SUBLANES - (CONV_WIDTH - 1):SUBLANES, :] = cst_ref[g]
        uext_s[SUBLANES:, :] = u
        u1 = uext_s[pl.ds(SUBLANES - 1, t), :]
        u2 = uext_s[pl.ds(SUBLANES - 2, t), :]
        oc = cb_ref[g] * (cw[0:1, :] * u2 + cw[1:2, :] * u1 + cw[2:3, :] * u)
        nst_ref[g] = uext_s[pl.ds(SUBLANES + t - (CONV_WIDTH - 1), CONV_WIDTH - 1), :]
        mix_ref[g] = jnp.concatenate([oa_ref[g], ob, oc], axis=1)


def _sample_rest(layer, sinks, oa, proj, caches, tabb, cw, prev_outs):
    cbk, cbv, cst = caches
    n, t = oa.shape[0], oa.shape[1]
    g = SEQS_PER_STEP
    proj3 = [a.reshape(n, t, a.shape[1]) for a in proj]
    per_seq = lambda a: pl.BlockSpec((g,) + a.shape[1:], lambda i: (i, 0, 0))
    per_layer_seq = lambda a: pl.BlockSpec((None, g) + a.shape[2:], lambda i: (layer, i, 0, 0))
    const = lambda a: pl.BlockSpec(a.shape, lambda i: (0, 0))
    n_alias = 0 if prev_outs is None else len(prev_outs)
    args = [sinks, oa] + proj3 + [cbk, cbv, cst, tabb, cw] + (list(prev_outs) if n_alias else [])
    in_specs = ([pl.BlockSpec(memory_space=pltpu.SMEM), per_seq(oa)] + [per_seq(a) for a in proj3]
                + [per_layer_seq(a) for a in caches] + [const(tabb), const(cw)]
                + [pl.BlockSpec(memory_space=pl.ANY)] * n_alias)
    first_alias = len(args) - n_alias
    outs = pl.pallas_call(
        functools.partial(_sample_rest_kernel, n_alias),
        grid=(n // g,),
        in_specs=in_specs,
        out_specs=[pl.BlockSpec((g, t, 4 * C_WIDTH), lambda i: (i, 0, 0)), per_layer_seq(cbk), per_layer_seq(cbv),
                   pl.BlockSpec((g, CONV_WIDTH - 1, C_WIDTH), lambda i: (i, 0, 0))],
        out_shape=[jax.ShapeDtypeStruct((n, t, 4 * C_WIDTH), f32), jax.ShapeDtypeStruct(cbk.shape, f32),
                   jax.ShapeDtypeStruct(cbv.shape, f32), jax.ShapeDtypeStruct((n, CONV_WIDTH - 1, C_WIDTH), f32)],
        scratch_shapes=[pltpu.VMEM((SUBLANES + t, C_WIDTH), f32)],
        input_output_aliases={first_alias + i: 1 + i for i in range(n_alias)},
        compiler_params=pltpu.CompilerParams(dimension_semantics=("arbitrary",),
                                             vmem_limit_bytes=VMEM_LIMIT),
        name=f"sample_rest_l{layer}",
    )(*args)
    mix, nbk, nbv, nst = outs
    return mix.reshape(n * t, 4 * C_WIDTH), (nbk, nbv), nst


def _t5_bucket(dist):
    dist = jnp.maximum(dist, 0)
    max_exact = NUM_BUCKETS // 2
    scaled = jnp.log(jnp.maximum(dist, 1).astype(f32) / max_exact) / math.log(MAX_DISTANCE / max_exact)
    large = max_exact + (scaled * (NUM_BUCKETS - max_exact)).astype(jnp.int32)
    large = jnp.minimum(large, NUM_BUCKETS - 1)
    return jnp.where(dist < max_exact, dist, large)


def _bias_of(table, dist):
    onehot = jax.nn.one_hot(_t5_bucket(dist), NUM_BUCKETS, dtype=f32)
    return jnp.einsum('...b,bh->h...', onehot, table.astype(f32), precision=lax.Precision.HIGHEST)


def _band_bias(table, d):
    qi = jnp.arange(BLOCK)[:, None]
    kj = jnp.arange(2 * BLOCK)[None, :]
    delta = BLOCK + qi - kj
    in_band = (delta >= 0) & (delta <= BLOCK)
    bias = _bias_of(table, delta * d)
    with_prev = jnp.where(in_band[None], bias, NEG)
    no_prev = jnp.where((in_band & (kj >= BLOCK))[None], bias, NEG)
    return jnp.stack([no_prev, with_prev], axis=0)


def _sample_table_a(table, t, buf, cols):
    i = jnp.arange(t)[:, None]
    r = jnp.arange(cols)[None, :]
    delta = buf + i - r
    count = sum(((delta >= 0) & (delta % d == 0) & (delta // d <= w // d)).astype(f32) for w, d in A_PATTERNS)
    tab = jnp.where((count > 0)[None], _bias_of(table, delta) + jnp.log(jnp.maximum(count, 1.0))[None], NEG)
    return tab.reshape(table.shape[1] * t, cols)


def _sample_table_b(table, t, buf, cols):
    i = jnp.arange(t)[:, None]
    r = jnp.arange(cols)[None, :]
    delta = buf + i - r
    valid = (delta >= 0) & (delta <= B_WINDOW)
    return jnp.where(valid[None], _bias_of(table, delta), NEG).reshape(table.shape[1] * t, cols)


def _expand_matrix():
    lane = jnp.arange(LANES)[:, None]
    col_head = jnp.arange(A_WIDTH)[None, :] // HEAD_DIM
    e = (lane == col_head).astype(bf16)
    return jnp.concatenate([e, e], axis=0)


def kernel(x_prompt, x_sample, cache_a_k, cache_a_v, cache_b_k, cache_b_v, state_conv, w_in, w_out, conv_w,
           b_sinks, rel_bias, g_mix_pre, g_mix_post, w_up, w_down, g_mlp_pre, g_mlp_post):
    depth = w_in.shape[0]
    batch, seq, d_model = x_prompt.shape
    n_dec, t_dec, _ = x_sample.shape
    buf_a, buf_b = cache_a_k.shape[2], cache_b_k.shape[2]
    assert batch == 1 and seq % (A_PATTERNS[-1][1] * BLOCK) == 0 and seq >= A_WINDOW
    assert buf_a == A_WINDOW and buf_b == B_WINDOW and t_dec == SUBLANES
    assert (n_dec * t_dec) % ROW_TILE == 0 and seq % ROW_TILE == 0 and n_dec % SEQS_PER_STEP == 0
    assert tuple(d for _, d in A_PATTERNS) == (1, GROUP, GROUP * GROUP) and seq % (GROUP * GROUP * Q_TILE) == 0

    xp = x_prompt.reshape(seq, d_model)
    xs = x_sample.reshape(n_dec * t_dec, d_model)
    to_fm = lambda c: jnp.transpose(c, (0, 1, 3, 4, 2)).reshape(depth, n_dec, c.shape[3] * HEAD_DIM, c.shape[2])
    from_fm = lambda c, heads: jnp.transpose(c.reshape(c.shape[:2] + (heads, HEAD_DIM, c.shape[3])), (0, 1, 4, 2, 3))
    caches_a = (to_fm(cache_a_k), to_fm(cache_a_v))
    caches_b = (to_fm(cache_b_k), to_fm(cache_b_v), state_conv)
    table_a, table_b = rel_bias[:, :HEADS], rel_bias[:, HEADS:]
    band_a = [_band_bias(table_a, d) for _, d in A_PATTERNS]
    band_b = _band_bias(table_b, 1)
    tab_sa = _sample_table_a(table_a, t_dec, buf_a, buf_a + LANES)
    tab_sb = _sample_table_b(table_b, t_dec, buf_b, buf_b + LANES)
    expand = _expand_matrix()
    row = lambda v: v.reshape(1, -1).astype(f32)
    gains = lambda v: v.reshape(depth, 1, -1).astype(f32)
    w_in_b = w_in.astype(bf16)
    params = (w_out.astype(bf16), gains(g_mix_post), gains(g_mlp_pre), w_up.astype(bf16), w_down.astype(bf16),
              gains(g_mlp_post))

    new_p = [[] for _ in range(5)]
    conv_s = []
    new_a = new_b = None
    for l in range(depth):
        cw = conv_w[l].astype(f32)
        sinks = b_sinks[l].astype(f32)

        (qa1, qa4, qa16, ka1, ka4, ka16, va1, va4, va16, qb, kbx, vbx, oc,
         kat, vat, kbt, vbt, ut) = _prompt_proj(l, xp, row(g_mix_pre[l]), w_in_b, cw)
        pats = [_band_attention(q, k, v, bias) for q, k, v, bias in
                ((qa1, ka1, va1, band_a[0]), (qa4, ka4, va4, band_a[1]), (qa16, ka16, va16, band_a[2]))]
        ob = _band_attention(qb, kbx, vbx, band_b, sinks=sinks)
        for lst, a in zip(new_p, (kat[None, None], vat[None, None], kbt[None, None], vbt[None, None],
                                  ut.reshape(1, CONV_WIDTH - 1, C_WIDTH))):
            lst.append(a)

        proj_s = _sample_proj(l, xs, row(g_mix_pre[l]), w_in_b)
        qkv_s = [a.reshape(n_dec, t_dec, A_WIDTH) for a in proj_s[:3]]
        xp, oa_s, new_a = _prompt_finish(l, xp, [p[0] for p in pats] + [p[1] for p in pats] + [ob, oc], qkv_s,
                                         expand, tab_sa, params, caches_a, new_a)

        mix_s, new_b, nst = _sample_rest(l, sinks, oa_s, proj_s[3:], caches_b, tab_sb, cw, new_b)
        conv_s.append(nst)
        xs = _sample_finish(l, xs, mix_s, params)

    a_k_p, a_v_p, b_k_p, b_v_p = [jnp.concatenate(t, axis=0) for t in new_p[:4]]
    return (xp.reshape(batch, seq, d_model), xs.reshape(n_dec, t_dec, d_model),
            from_fm(a_k_p, HEADS), from_fm(a_v_p, HEADS), from_fm(b_k_p, 2), from_fm(b_v_p, 2),
            jnp.stack(new_p[4], axis=0),
            from_fm(new_a[0], HEADS), from_fm(new_a[1], HEADS), from_fm(new_b[0], 2), from_fm(new_b[1], 2),
            jnp.stack(conv_s, axis=0))
```

```python
import functools
import math

import jax
import jax.numpy as jnp
from jax import lax
from jax.experimental import pallas as pl
from jax.experimental.pallas import tpu as pltpu

HEAD_DIM = 64
HEADS = 6
A_WIDTH = HEADS * HEAD_DIM
B_KV_WIDTH = 2 * HEAD_DIM
C_WIDTH = 256
A_PATTERNS = ((128, 1), (512, 4), (2048, 16))
A_WINDOW = 2048
B_WINDOW = 128
BLOCK = 128
CONV_WIDTH = 3
NUM_BUCKETS = 32
MAX_DISTANCE = 2048
EPS = 1e-6
SCALE = 1.0 / math.sqrt(HEAD_DIM)
NEG = -1e30
LANES = 128
SUBLANES = 8
ROW_TILE = 512
Q_TILE = 1024
GROUP = 4
FF_CHUNK = 1024
SEQS_PER_STEP = 16
READ_AHEAD = 2
READ_SLOTS = READ_AHEAD + 1
WRITE_SLOTS = 2
VMEM_LIMIT = 56 * 1024 * 1024
VMEM_LIMIT_STREAMING = 60 * 1024 * 1024

_IN_SIZES = (A_WIDTH, A_WIDTH, A_WIDTH, A_WIDTH, B_KV_WIDTH, B_KV_WIDTH, C_WIDTH, C_WIDTH, C_WIDTH)
_IN_OFFS = tuple(sum(_IN_SIZES[:i]) for i in range(len(_IN_SIZES) + 1))

f32 = jnp.float32
bf16 = jnp.bfloat16


def _rms(x, g):
    ms = jnp.mean(x * x, axis=-1, keepdims=True)
    return (x * lax.rsqrt(ms + EPS)) * g


def _dot(a, b):
    return jnp.dot(a, b, preferred_element_type=f32)


def _dot_nt(a, b):
    return lax.dot_general(a, b, (((1,), (1,)), ((), ())), preferred_element_type=f32)


def _project(x_ref, g_ref, w_ref):
    hb = _rms(x_ref[...], g_ref[...]).astype(bf16)
    proj = _dot(hb, w_ref[...])
    return [proj[:, _IN_OFFS[i]:_IN_OFFS[i + 1]] for i in range(len(_IN_SIZES))]


def _swap_halves(t):
    return pltpu.roll(t, HEAD_DIM, axis=1)


def _low_half(shape):
    return lax.broadcasted_iota(jnp.int32, shape, 1) < HEAD_DIM


def _layer_block(a, layer):
    return pl.BlockSpec((None,) + a.shape[1:], lambda *_: (layer, 0, 0), pipeline_mode=pl.Buffered(1))


def _write_grouped(x, nat_ref, g4_ref, g16_ref, nat_s, g4_s):
    t, w = x.shape
    slabs = w // LANES
    nat_ref[...] = x.astype(bf16)
    for s in range(slabs):
        nat_s[s] = x[:, s * LANES:(s + 1) * LANES]
    for r in range(GROUP):
        for s in range(slabs):
            g = nat_s[s, pl.ds(r, t // GROUP, stride=GROUP), :]
            g4_s[r * slabs + s] = g
            g4_ref[:, (r * slabs + s) * LANES:(r * slabs + s + 1) * LANES] = g.astype(bf16)
    for c in range(GROUP * GROUP):
        r, r2 = c % GROUP, c // GROUP
        for s in range(slabs):
            h = g4_s[r * slabs + s, pl.ds(r2, t // (GROUP * GROUP), stride=GROUP), :]
            g16_ref[:, (c * slabs + s) * LANES:(c * slabs + s + 1) * LANES] = h.astype(bf16)


def _prompt_proj_kernel(n_steps, x_ref, g_ref, w_ref, cw_ref,
                        qa1_ref, qa4_ref, qa16_ref, ka1_ref, ka4_ref, ka16_ref, va1_ref, va4_ref, va16_ref,
                        qb_ref, kbx_ref, vbx_ref, oc_ref,
                        kat_ref, vat_ref, kbt_ref, vbt_ref, ut_ref, uext_ref, nat_s, g4_s):
    t = x_ref.shape[0]
    step = pl.program_id(0)
    aq, ak, av, bq, bk, bv, cb, cc, cx = _project(x_ref, g_ref, w_ref)
    _write_grouped(aq * SCALE, qa1_ref, qa4_ref, qa16_ref, nat_s, g4_s)
    _write_grouped(ak, ka1_ref, ka4_ref, ka16_ref, nat_s, g4_s)
    _write_grouped(av, va1_ref, va4_ref, va16_ref, nat_s, g4_s)
    qb_ref[...] = (bq * SCALE).astype(bf16)
    low = _low_half(bk.shape)

    def widen(kv):
        sw = _swap_halves(kv)
        return jnp.concatenate([jnp.where(low, kv, sw), kv, jnp.where(low, sw, kv)], axis=1)

    kbx_ref[...] = widen(bk).astype(bf16)
    vbx_ref[...] = widen(bv).astype(bf16)

    @pl.when(step >= n_steps - A_WINDOW // t)
    def _():
        kat_ref[...] = ak.T
        vat_ref[...] = av.T

    @pl.when(step == 0)
    def _():
        uext_ref[0:SUBLANES, :] = jnp.zeros((SUBLANES, C_WIDTH), f32)

    u = cc * cx
    uext_ref[SUBLANES:, :] = u
    u1 = uext_ref[pl.ds(SUBLANES - 1, t), :]
    u2 = uext_ref[pl.ds(SUBLANES - 2, t), :]
    cw = cw_ref[...]
    conv = cw[0:1, :] * u2 + cw[1:2, :] * u1 + cw[2:3, :] * u
    oc_ref[...] = (cb * conv).astype(bf16)

    @pl.when(step == n_steps - 1)
    def _():
        kbt_ref[...] = bk[t - B_WINDOW:, :].T
        vbt_ref[...] = bv[t - B_WINDOW:, :].T
        ut_ref[...] = uext_ref[pl.ds(t + SUBLANES - (CONV_WIDTH - 1), CONV_WIDTH - 1), :]

    uext_ref[0:SUBLANES, :] = u[t - SUBLANES:, :]


def _prompt_proj(layer, x, g, w, cw):
    s, d = x.shape
    t = ROW_TILE
    n = s // t
    tail_blocks = A_WINDOW // t
    row = lambda width: pl.BlockSpec((t, width), lambda i: (i, 0))
    const = lambda shape: pl.BlockSpec(shape, lambda i: (0, 0))
    tail = pl.BlockSpec((A_WIDTH, t), lambda i: (0, jnp.maximum(i - (n - tail_blocks), 0)))
    grouped_specs, grouped_shapes = [], []
    for _ in range(3):
        for _, dil in A_PATTERNS:
            grouped_specs.append(pl.BlockSpec((t // dil, dil * A_WIDTH), lambda i: (i, 0)))
            grouped_shapes.append(jax.ShapeDtypeStruct((s // dil, dil * A_WIDTH), bf16))
    bfo = lambda width: jax.ShapeDtypeStruct((s, width), bf16)
    return pl.pallas_call(
        functools.partial(_prompt_proj_kernel, n),
        grid=(n,),
        in_specs=[row(d), const((1, d)), _layer_block(w, layer), const(cw.shape)],
        out_specs=grouped_specs + [row(A_WIDTH)] * 3 + [row(C_WIDTH), tail, tail,
                                                        const((B_KV_WIDTH, B_WINDOW)), const((B_KV_WIDTH, B_WINDOW)),
                                                        const((CONV_WIDTH - 1, C_WIDTH))],
        out_shape=grouped_shapes + [bfo(A_WIDTH)] * 3 + [bfo(C_WIDTH),
                                                         jax.ShapeDtypeStruct((A_WIDTH, A_WINDOW), f32),
                                                         jax.ShapeDtypeStruct((A_WIDTH, A_WINDOW), f32),
                                                         jax.ShapeDtypeStruct((B_KV_WIDTH, B_WINDOW), f32),
                                                         jax.ShapeDtypeStruct((B_KV_WIDTH, B_WINDOW), f32),
                                                         jax.ShapeDtypeStruct((CONV_WIDTH - 1, C_WIDTH), f32)],
        scratch_shapes=[pltpu.VMEM((t + SUBLANES, C_WIDTH), f32),
                        pltpu.VMEM((A_WIDTH // LANES, t, LANES), f32),
                        pltpu.VMEM((GROUP * A_WIDTH // LANES, t // GROUP, LANES), f32)],
        compiler_params=pltpu.CompilerParams(dimension_semantics=("arbitrary",),
                                             vmem_limit_bytes=VMEM_LIMIT),
        name="prompt_proj",
    )(x, g, w, cw)


def _band_attn_kernel(gated, *refs):
    if gated:
        sink_ref, q_ref, kp_ref, kc_ref, vp_ref, vc_ref, bias_ref, o_ref = refs
    else:
        q_ref, kp_ref, kc_ref, vp_ref, vc_ref, bias_ref, o_ref, lse_ref = refs
    has_prev = jnp.minimum(pl.program_id(1), 1)
    kcat = jnp.concatenate([kp_ref[...], kc_ref[...]], axis=0)
    vcat = jnp.concatenate([vp_ref[...], vc_ref[...]], axis=0)
    low = _low_half((BLOCK, LANES))
    lane = lax.broadcasted_iota(jnp.int32, (BLOCK, LANES), 1)
    zero = jnp.zeros((BLOCK, LANES), bf16)
    for b in range(q_ref.shape[0] // BLOCK):
        rows = slice(b * BLOCK, (b + 1) * BLOCK)
        variant = has_prev if b == 0 else 1
        lse_tile = jnp.zeros((BLOCK, LANES), f32)
        for p in range(HEADS // 2):
            cols = slice(p * LANES, (p + 1) * LANES)
            qt = q_ref[rows, cols]
            kt = kcat[b * BLOCK:(b + 2) * BLOCK, cols]
            vt = vcat[b * BLOCK:(b + 2) * BLOCK, cols]
            halves = []
            for e in range(2):
                h = 2 * p + e
                qm = jnp.where(low if e == 0 else ~low, qt, zero)
                s = _dot_nt(qm, kt) + bias_ref[variant, h]
                m = jnp.max(s, axis=-1, keepdims=True)
                pe = jnp.exp(s - m)
                l = jnp.sum(pe, axis=-1, keepdims=True)
                o = _dot(pe.astype(bf16), vt) / l
                lse = m + jnp.log(l)
                if gated:
                    o = o * jax.nn.sigmoid(lse - sink_ref[h])
                else:
                    lse_tile = jnp.where(lane == h, lse, lse_tile)
                halves.append(o)
            o_ref[rows, cols] = jnp.where(low, halves[0], halves[1]).astype(bf16)
        if not gated:
            lse_ref[rows, :] = lse_tile


def _band_attention(q, k, v, bias, sinks=None):
    rows, width = q.shape
    d = width // A_WIDTH
    per_tile = Q_TILE // BLOCK
    cur = pl.BlockSpec((Q_TILE, A_WIDTH), lambda c, j: (j, c))
    prev = pl.BlockSpec((BLOCK, A_WIDTH), lambda c, j: (jnp.maximum(j * per_tile - 1, 0), c))
    bias_spec = pl.BlockSpec(bias.shape, lambda c, j: (0, 0, 0, 0))
    gated = sinks is not None
    in_specs = [cur, prev, cur, prev, cur, bias_spec]
    args = [q, k, k, v, v, bias]
    out_specs = [cur]
    out_shape = [jax.ShapeDtypeStruct((rows, width), bf16)]
    if gated:
        in_specs = [pl.BlockSpec(memory_space=pltpu.SMEM)] + in_specs
        args = [sinks] + args
    else:
        out_specs.append(pl.BlockSpec((Q_TILE, LANES), lambda c, j: (j, c)))
        out_shape.append(jax.ShapeDtypeStruct((rows, d * LANES), f32))
    outs = pl.pallas_call(
        functools.partial(_band_attn_kernel, gated),
        grid=(d, rows // Q_TILE),
        in_specs=in_specs, out_specs=out_specs, out_shape=out_shape,
        compiler_params=pltpu.CompilerParams(dimension_semantics=("arbitrary", "arbitrary"),
                                             vmem_limit_bytes=VMEM_LIMIT),
        name=f"band_attn_d{d}" + ("_gated" if gated else ""),
    )(*args)
    return outs[0] if gated else tuple(outs)


def _pad_rows(new, at_end):
    z = jnp.zeros((LANES - new.shape[0], new.shape[1]), f32)
    return jnp.concatenate([z, new] if at_end else [new, z], axis=0)


def _softmax_rows(s):
    m = jnp.max(s, axis=-1, keepdims=True)
    p = jnp.exp(s - m)
    l = jnp.sum(p, axis=-1, keepdims=True)
    return p, m, l


def _cache_scores(qbd, kt_ref, k_new, tab_ref):
    s = jnp.concatenate([_dot(qbd, kt_ref[...].astype(bf16)),
                         _dot_nt(qbd, _pad_rows(k_new, False).astype(bf16))], axis=1) + tab_ref[...]
    p, m, l = _softmax_rows(s)
    return p.astype(bf16), m, l


def _cache_values(pb, l, vt_ref, v_new):
    buf = vt_ref.shape[1]
    res = _dot_nt(pb[:, :buf], vt_ref[...].astype(bf16)) + _dot(pb[:, buf:], _pad_rows(v_new, False).astype(bf16))
    return res / l


def _shift_into(dst_ref, ct_ref, new):
    w, buf = ct_ref.shape
    t = new.shape[0]
    new_t = jnp.transpose(_pad_rows(new, True))
    rolled = pltpu.roll(ct_ref[...], buf - t, axis=1)
    if buf > LANES:
        dst_ref[:, 0:buf - LANES] = rolled[:, 0:buf - LANES]
    lane = lax.broadcasted_iota(jnp.int32, (w, LANES), 1)
    dst_ref[:, buf - LANES:] = jnp.where(lane >= LANES - t, new_t, rolled[:, buf - LANES:])


def _own_head_mask(t):
    rows = HEADS * t
    row_head = lax.broadcasted_iota(jnp.int32, (rows, A_WIDTH), 0) >> (t.bit_length() - 1)
    col_head = lax.broadcasted_iota(jnp.int32, (rows, A_WIDTH), 1) >> (HEAD_DIM.bit_length() - 1)
    return row_head == col_head


def _out_proj_stage(x, mix_b, wo_ref, gpost_ref, gpre2_ref):
    x1 = x + _rms(_dot(mix_b, wo_ref[...]), gpost_ref[...])
    return x1, _rms(x1, gpre2_ref[...]).astype(bf16)


def _ffn_chunk_stage(hb, acc, wup_ref, wdn_ref, c):
    cols = slice(c * FF_CHUNK, (c + 1) * FF_CHUNK)
    a = jnp.maximum(_dot(hb, wup_ref[:, cols]), 0.0)
    return acc + _dot((a * a).astype(bf16), wdn_ref[cols, :])


def _ungroup(blk_ref, w, nat_s, g4_s):
    rows, width = blk_ref.shape
    slabs, d = w // LANES, width // w
    if d == 1:
        return blk_ref[...].astype(f32)
    if d == GROUP * GROUP:
        for c in range(d):
            r, r2 = c % GROUP, c // GROUP
            for s in range(slabs):
                col = (c * slabs + s) * LANES
                g4_s[r * slabs + s, pl.ds(r2, rows, stride=GROUP), :] = blk_ref[:, col:col + LANES].astype(f32)
        for r in range(GROUP):
            for s in range(slabs):
                nat_s[s, pl.ds(r, rows * GROUP, stride=GROUP), :] = g4_s[r * slabs + s]
    else:
        assert d == GROUP
        for r in range(GROUP):
            for s in range(slabs):
                col = (r * slabs + s) * LANES
                nat_s[s, pl.ds(r, rows, stride=GROUP), :] = blk_ref[:, col:col + LANES].astype(f32)
    return jnp.concatenate([nat_s[s] for s in range(slabs)], axis=1)


def _prompt_finish_kernel(layer, n_alias, x_ref, o1_ref, o4_ref, o16_ref, l1_ref, l4_ref, l16_ref, ob_ref, oc_ref,
                          qa_ref, ka_ref, va_ref, exp_ref, taba_ref, wo_ref, gpost_ref, gpre2_ref, wup_ref, wdn_ref,
                          gpost2_ref, cak_hbm, cav_hbm, *rest):
    out_ref, oa_ref, nak_hbm, nav_hbm, nat_s, g4_s, in_buf, out_buf, in_sem, out_sem = rest[n_alias:]
    step, n_steps = pl.program_id(0), pl.num_programs(0)
    nseq, t = qa_ref.shape[0], qa_ref.shape[1]
    n_units = 2 * nseq
    own = _own_head_mask(t)
    st = {}
    base = step * nseq
    first_slot = lax.rem(step * n_units, READ_SLOTS)

    def read_slot(u):
        return lax.rem(first_slot + u, READ_SLOTS)

    def read(u):
        slot = read_slot(u)
        src = (cak_hbm, cav_hbm)[u % 2].at[layer, base + u // 2]
        return pltpu.make_async_copy(src, in_buf.at[slot], in_sem.at[slot])

    def write(u):
        dst = (nak_hbm, nav_hbm)[u % 2].at[layer, base + u // 2]
        return pltpu.make_async_copy(out_buf.at[u % 2], dst, out_sem.at[u % 2])

    def merge_stage():
        lses = [_ungroup(l_ref, LANES, nat_s, g4_s) for l_ref in (l1_ref, l4_ref, l16_ref)]
        m = jnp.maximum(jnp.maximum(lses[0], lses[1]), lses[2])
        es = [jnp.exp(l - m) for l in lses]
        tot = es[0] + es[1] + es[2]
        oa = jnp.zeros((x_ref.shape[0], A_WIDTH), f32)
        for e, o_ref in zip(es, (o1_ref, o4_ref, o16_ref)):
            w = e / tot
            w_hi = w.astype(bf16)
            w_lo = (w - w_hi.astype(f32)).astype(bf16)
            wx = _dot(jnp.concatenate([w_hi, w_lo], axis=1), exp_ref[...])
            oa = oa + wx * _ungroup(o_ref, A_WIDTH, nat_s, g4_s)
        st["mix"] = jnp.concatenate([oa.astype(bf16), ob_ref[...], oc_ref[...]], axis=1)

    def proj_stage():
        st["x1"], st["hb"] = _out_proj_stage(x_ref[...], st["mix"], wo_ref, gpost_ref, gpre2_ref)
        st["acc"] = jnp.zeros(x_ref.shape, f32)

    def ffn_stage(c):
        st["acc"] = _ffn_chunk_stage(st["hb"], st["acc"], wup_ref, wdn_ref, c)

    def final_stage():
        out_ref[...] = st["x1"] + _rms(st["acc"], gpost2_ref[...])

    stages = ([merge_stage, proj_stage]
              + [functools.partial(ffn_stage, c) for c in range(wup_ref.shape[1] // FF_CHUNK)] + [final_stage])

    def keys_unit(s, src_ref):
        q6 = jnp.concatenate([qa_ref[s]] * HEADS, axis=0)
        qbd = jnp.where(own, q6, 0.0).astype(bf16)
        st["p"], _, st["l"] = _cache_scores(qbd, src_ref, ka_ref[s], taba_ref)
        _shift_into(out_buf.at[0], src_ref, ka_ref[s])

    def values_unit(s, src_ref):
        res = jnp.where(own, _cache_values(st["p"], st["l"], src_ref, va_ref[s]), 0.0)
        oa = res[0:t, :]
        for h in range(1, HEADS):
            oa = oa + res[h * t:(h + 1) * t, :]
        oa_ref[s] = oa
        _shift_into(out_buf.at[1], src_ref, va_ref[s])

    @pl.when(step == 0)
    def _():
        for u in range(READ_AHEAD):
            read(u).start()

    for u in range(n_units):
        if u + READ_AHEAD < n_units:
            read(u + READ_AHEAD).start()
        else:
            @pl.when(step + 1 < n_steps)
            def _():
                read(u + READ_AHEAD).start()
        read(u).wait()
        if u >= 2:
            write(u - 2).wait()
        else:
            @pl.when(step > 0)
            def _():
                write(u).wait()
        if u < len(stages):
            stages[u]()
        (keys_unit if u % 2 == 0 else values_unit)(u // 2, in_buf.at[read_slot(u)])
        write(u).start()
    for stage in stages[n_units:]:
        stage()

    @pl.when(step == n_steps - 1)
    def _():
        write(n_units - 2).wait()
        write(n_units - 1).wait()


def _prompt_finish(layer, x, row_inputs, sample_qkv, expand, taba, params, caches, prev_outs):
    s, d = x.shape
    t = ROW_TILE
    n_steps = s // t
    n_seq = sample_qkv[0].shape[0]
    assert n_seq % n_steps == 0
    per_step = n_seq // n_steps
    row = lambda a: pl.BlockSpec((a.shape[0] * t // s, a.shape[1]), lambda i: (i, 0))
    seqs = lambda a: pl.BlockSpec((per_step,) + a.shape[1:], lambda i: (i, 0, 0))
    const = lambda a: pl.BlockSpec(a.shape, lambda i: (0, 0), pipeline_mode=pl.Buffered(1))
    gain = lambda a: pl.BlockSpec((None, 1, a.shape[2]), lambda i: (layer, 0, 0))
    w_out, g_post, g_pre2, w_up, w_down, g_post2 = params
    cak, cav = caches
    n_alias = 0 if prev_outs is None else len(prev_outs)
    any_spec = pl.BlockSpec(memory_space=pl.ANY)
    args = ([x] + row_inputs + list(sample_qkv)
            + [expand, taba, w_out, g_post, g_pre2, w_up, w_down, g_post2, cak, cav]
            + (list(prev_outs) if n_alias else []))
    in_specs = ([row(a) for a in [x] + row_inputs] + [seqs(a) for a in sample_qkv]
                + [const(expand), const(taba), _layer_block(w_out, layer), gain(g_post), gain(g_pre2),
                   _layer_block(w_up, layer), _layer_block(w_down, layer), gain(g_post2), any_spec, any_spec]
                + [any_spec] * n_alias)
    first_alias = len(args) - n_alias
    assert (2 * per_step) % WRITE_SLOTS == 0
    outs = pl.pallas_call(
        functools.partial(_prompt_finish_kernel, layer, n_alias),
        grid=(n_steps,),
        in_specs=in_specs,
        out_specs=[pl.BlockSpec((t, d), lambda i: (i, 0)), seqs(sample_qkv[0]), any_spec, any_spec],
        out_shape=[jax.ShapeDtypeStruct((s, d), f32), jax.ShapeDtypeStruct(sample_qkv[0].shape, f32),
                   jax.ShapeDtypeStruct(cak.shape, f32), jax.ShapeDtypeStruct(cav.shape, f32)],
        scratch_shapes=[pltpu.VMEM((A_WIDTH // LANES, t, LANES), f32),
                        pltpu.VMEM((GROUP * A_WIDTH // LANES, t // GROUP, LANES), f32),
                        pltpu.VMEM((READ_SLOTS,) + cak.shape[2:], f32), pltpu.VMEM((WRITE_SLOTS,) + cak.shape[2:], f32),
                        pltpu.SemaphoreType.DMA((READ_SLOTS,)), pltpu.SemaphoreType.DMA((WRITE_SLOTS,))],
        input_output_aliases={first_alias + i: 2 + i for i in range(n_alias)},
        compiler_params=pltpu.CompilerParams(dimension_semantics=("arbitrary",),
                                             vmem_limit_bytes=VMEM_LIMIT_STREAMING),
        name=f"prompt_finish_l{layer}",
    )(*args)
    return outs[0], outs[1], (outs[2], outs[3])


def _sample_finish_kernel(x_ref, mix_ref, wo_ref, gpost_ref, gpre2_ref, wup_ref, wdn_ref, gpost2_ref, out_ref):
    x1, hb = _out_proj_stage(x_ref[...], mix_ref[...].astype(bf16), wo_ref, gpost_ref, gpre2_ref)
    acc = jnp.zeros(x1.shape, f32)
    for c in range(wup_ref.shape[1] // FF_CHUNK):
        acc = _ffn_chunk_stage(hb, acc, wup_ref, wdn_ref, c)
    out_ref[...] = x1 + _rms(acc, gpost2_ref[...])


def _sample_finish(layer, x, mix, params):
    s, d = x.shape
    t = ROW_TILE
    row = lambda a: pl.BlockSpec((t, a.shape[1]), lambda i: (i, 0))
    gain = lambda a: pl.BlockSpec((None, 1, a.shape[2]), lambda i: (layer, 0, 0))
    w_out, g_post, g_pre2, w_up, w_down, g_post2 = params
    return pl.pallas_call(
        _sample_finish_kernel,
        grid=(s // t,),
        in_specs=[row(x), row(mix), _layer_block(w_out, layer), gain(g_post), gain(g_pre2),
                  _layer_block(w_up, layer), _layer_block(w_down, layer), gain(g_post2)],
        out_specs=pl.BlockSpec((t, d), lambda i: (i, 0)),
        out_shape=jax.ShapeDtypeStruct((s, d), f32),
        compiler_params=pltpu.CompilerParams(dimension_semantics=("arbitrary",),
                                             vmem_limit_bytes=VMEM_LIMIT),
        name="sample_finish",
    )(x, mix, w_out, g_post, g_pre2, w_up, w_down, g_post2)


def _sample_proj_kernel(x_ref, g_ref, w_ref, qa_ref, ka_ref, va_ref, qb_ref, kb_ref, vb_ref, cb_ref, u_ref):
    aq, ak, av, bq, bk, bv, cb, cc, cx = _project(x_ref, g_ref, w_ref)
    qa_ref[...] = aq * SCALE
    ka_ref[...] = ak
    va_ref[...] = av
    qb_ref[...] = bq * SCALE
    kb_ref[...] = bk
    vb_ref[...] = bv
    cb_ref[...] = cb
    u_ref[...] = cc * cx


def _sample_proj(layer, x, g, w):
    s, d = x.shape
    t = ROW_TILE
    widths = (A_WIDTH, A_WIDTH, A_WIDTH, A_WIDTH, B_KV_WIDTH, B_KV_WIDTH, C_WIDTH, C_WIDTH)
    row = lambda width: pl.BlockSpec((t, width), lambda i: (i, 0))
    const = lambda shape: pl.BlockSpec(shape, lambda i: (0, 0))
    return pl.pallas_call(
        _sample_proj_kernel,
        grid=(s // t,),
        in_specs=[row(d), const((1, d)), _layer_block(w, layer)],
        out_specs=[row(wd) for wd in widths],
        out_shape=[jax.ShapeDtypeStruct((s, wd), f32) for wd in widths],
        compiler_params=pltpu.CompilerParams(dimension_semantics=("arbitrary",),
                                             vmem_limit_bytes=VMEM_LIMIT),
        name="sample_proj",
    )(x, g, w)


def _sample_rest_kernel(n_alias, sink_ref, oa_ref, qb_ref, kb_ref, vb_ref, cb_ref, u_ref,
                        cbk_ref, cbv_ref, cst_ref, tabb_ref, cw_ref, *rest):
    mix_ref, nbk_ref, nbv_ref, nst_ref, uext_s = rest[n_alias:]
    t = qb_ref.shape[1]
    rows = HEADS * t
    low = _low_half((t, LANES))
    zero = jnp.zeros((t, LANES), f32)
    row_head = lax.broadcasted_iota(jnp.int32, (rows, 1), 0) >> (t.bit_length() - 1)
    sink_col = jnp.zeros((rows, 1), f32)
    for h in range(HEADS):
        sink_col = jnp.where(row_head == h, sink_ref[h], sink_col)
    cw = cw_ref[...]
    for g in range(qb_ref.shape[0]):
        kb_new, vb_new = kb_ref[g], vb_ref[g]
        qb = qb_ref[g]
        t0, t1, t2 = (qb[:, i * LANES:(i + 1) * LANES] for i in range(3))
        qbd = jnp.concatenate([
            jnp.where(low, t0, zero), jnp.where(low, _swap_halves(t0), zero), jnp.where(low, t1, zero),
            jnp.where(low, zero, t1), jnp.where(low, zero, _swap_halves(t2)), jnp.where(low, zero, t2)],
            axis=0).astype(bf16)
        pb, mb, lb = _cache_scores(qbd, cbk_ref.at[g], kb_new, tabb_ref)
        resb = _cache_values(pb, lb, cbv_ref.at[g], vb_new) * jax.nn.sigmoid(mb + jnp.log(lb) - sink_col)
        r = [resb[h * t:(h + 1) * t, :] for h in range(HEADS)]
        ob = jnp.concatenate([jnp.where(low, r[0], _swap_halves(r[1])), jnp.where(low, r[2], r[3]),
                              jnp.where(low, _swap_halves(r[4]), r[5])], axis=1)
        _shift_into(nbk_ref.at[g], cbk_ref.at[g], kb_new)
        _shift_into(nbv_ref.at[g], cbv_ref.at[g], vb_new)

        u = u_ref[g]
        uext_s[SUBLANES - (CONV_WIDTH - 1):SUBLANES, :] = cst_ref[g]
        uext_s[SUBLANES:, :] = u
        u1 = uext_s[pl.ds(SUBLANES - 1, t), :]
        u2 = uext_s[pl.ds(SUBLANES - 2, t), :]
        oc = cb_ref[g] * (cw[0:1, :] * u2 + cw[1:2, :] * u1 + cw[2:3, :] * u)
        nst_ref[g] = uext_s[pl.ds(SUBLANES + t - (CONV_WIDTH - 1), CONV_WIDTH - 1), :]
        mix_ref[g] = jnp.concatenate([oa_ref[g], ob, oc], axis=1)


def _sample_rest(layer, sinks, oa, proj, caches, tabb, cw, prev_outs):
    cbk, cbv, cst = caches
    n, t = oa.shape[0], oa.shape[1]
    g = SEQS_PER_STEP
    proj3 = [a.reshape(n, t, a.shape[1]) for a in proj]
    per_seq = lambda a: pl.BlockSpec((g,) + a.shape[1:], lambda i: (i, 0, 0))
    per_layer_seq = lambda a: pl.BlockSpec((None, g) + a.shape[2:], lambda i: (layer, i, 0, 0))
    const = lambda a: pl.BlockSpec(a.shape, lambda i: (0, 0))
    n_alias = 0 if prev_outs is None else len(prev_outs)
    args = [sinks, oa] + proj3 + [cbk, cbv, cst, tabb, cw] + (list(prev_outs) if n_alias else [])
    in_specs = ([pl.BlockSpec(memory_space=pltpu.SMEM), per_seq(oa)] + [per_seq(a) for a in proj3]
                + [per_layer_seq(a) for a in caches] + [const(tabb), const(cw)]
                + [pl.BlockSpec(memory_space=pl.ANY)] * n_alias)
    first_alias = len(args) - n_alias
    outs = pl.pallas_call(
        functools.partial(_sample_rest_kernel, n_alias),
        grid=(n // g,),
        in_specs=in_specs,
        out_specs=[pl.BlockSpec((g, t, 4 * C_WIDTH), lambda i: (i, 0, 0)), per_layer_seq(cbk), per_layer_seq(cbv),
                   pl.BlockSpec((g, CONV_WIDTH - 1, C_WIDTH), lambda i: (i, 0, 0))],
        out_shape=[jax.ShapeDtypeStruct((n, t, 4 * C_WIDTH), f32), jax.ShapeDtypeStruct(cbk.shape, f32),
                   jax.ShapeDtypeStruct(cbv.shape, f32), jax.ShapeDtypeStruct((n, CONV_WIDTH - 1, C_WIDTH), f32)],
        scratch_shapes=[pltpu.VMEM((SUBLANES + t, C_WIDTH), f32)],
        input_output_aliases={first_alias + i: 1 + i for i in range(n_alias)},
        compiler_params=pltpu.CompilerParams(dimension_semantics=("arbitrary",),
                                             vmem_limit_bytes=VMEM_LIMIT),
        name=f"sample_rest_l{layer}",
    )(*args)
    mix, nbk, nbv, nst = outs
    return mix.reshape(n * t, 4 * C_WIDTH), (nbk, nbv), nst


def _t5_bucket(dist):
    dist = jnp.maximum(dist, 0)
    max_exact = NUM_BUCKETS // 2
    scaled = jnp.log(jnp.maximum(dist, 1).astype(f32) / max_exact) / math.log(MAX_DISTANCE / max_exact)
    large = max_exact + (scaled * (NUM_BUCKETS - max_exact)).astype(jnp.int32)
    large = jnp.minimum(large, NUM_BUCKETS - 1)
    return jnp.where(dist < max_exact, dist, large)


def _bias_of(table, dist):
    onehot = jax.nn.one_hot(_t5_bucket(dist), NUM_BUCKETS, dtype=f32)
    return jnp.einsum('...b,bh->h...', onehot, table.astype(f32), precision=lax.Precision.HIGHEST)


def _band_bias(table, d):
    qi = jnp.arange(BLOCK)[:, None]
    kj = jnp.arange(2 * BLOCK)[None, :]
    delta = BLOCK + qi - kj
    in_band = (delta >= 0) & (delta <= BLOCK)
    bias = _bias_of(table, delta * d)
    with_prev = jnp.where(in_band[None], bias, NEG)
    no_prev = jnp.where((in_band & (kj >= BLOCK))[None], bias, NEG)
    return jnp.stack([no_prev, with_prev], axis=0)


def _sample_table_a(table, t, buf, cols):
    i = jnp.arange(t)[:, None]
    r = jnp.arange(cols)[None, :]
    delta = buf + i - r
    count = sum(((delta >= 0) & (delta % d == 0) & (delta // d <= w // d)).astype(f32) for w, d in A_PATTERNS)
    tab = jnp.where((count > 0)[None], _bias_of(table, delta) + jnp.log(jnp.maximum(count, 1.0))[None], NEG)
    return tab.reshape(table.shape[1] * t, cols)


def _sample_table_b(table, t, buf, cols):
    i = jnp.arange(t)[:, None]
    r = jnp.arange(cols)[None, :]
    delta = buf + i - r
    valid = (delta >= 0) & (delta <= B_WINDOW)
    return jnp.where(valid[None], _bias_of(table, delta), NEG).reshape(table.shape[1] * t, cols)


def _expand_matrix():
    lane = jnp.arange(LANES)[:, None]
    col_head = jnp.arange(A_WIDTH)[None, :] // HEAD_DIM
    e = (lane == col_head).astype(bf16)
    return jnp.concatenate([e, e], axis=0)


def kernel(x_prompt, x_sample, cache_a_k, cache_a_v, cache_b_k, cache_b_v, state_conv, w_in, w_out, conv_w,
           b_sinks, rel_bias, g_mix_pre, g_mix_post, w_up, w_down, g_mlp_pre, g_mlp_post):
    depth = w_in.shape[0]
    batch, seq, d_model = x_prompt.shape
    n_dec, t_dec, _ = x_sample.shape
    buf_a, buf_b = cache_a_k.shape[2], cache_b_k.shape[2]
    assert batch == 1 and seq % (A_PATTERNS[-1][1] * BLOCK) == 0 and seq >= A_WINDOW
    assert buf_a == A_WINDOW and buf_b == B_WINDOW and t_dec == SUBLANES
    assert (n_dec * t_dec) % ROW_TILE == 0 and seq % ROW_TILE == 0 and n_dec % SEQS_PER_STEP == 0
    assert tuple(d for _, d in A_PATTERNS) == (1, GROUP, GROUP * GROUP) and seq % (GROUP * GROUP * Q_TILE) == 0

    xp = x_prompt.reshape(seq, d_model)
    xs = x_sample.reshape(n_dec * t_dec, d_model)
    to_fm = lambda c: jnp.transpose(c, (0, 1, 3, 4, 2)).reshape(depth, n_dec, c.shape[3] * HEAD_DIM, c.shape[2])
    from_fm = lambda c, heads: jnp.transpose(c.reshape(c.shape[:2] + (heads, HEAD_DIM, c.shape[3])), (0, 1, 4, 2, 3))
    caches_a = (to_fm(cache_a_k), to_fm(cache_a_v))
    caches_b = (to_fm(cache_b_k), to_fm(cache_b_v), state_conv)
    table_a, table_b = rel_bias[:, :HEADS], rel_bias[:, HEADS:]
    band_a = [_band_bias(table_a, d) for _, d in A_PATTERNS]
    band_b = _band_bias(table_b, 1)
    tab_sa = _sample_table_a(table_a, t_dec, buf_a, buf_a + LANES)
    tab_sb = _sample_table_b(table_b, t_dec, buf_b, buf_b + LANES)
    expand = _expand_matrix()
    row = lambda v: v.reshape(1, -1).astype(f32)
    gains = lambda v: v.reshape(depth, 1, -1).astype(f32)
    w_in_b = w_in.astype(bf16)
    params = (w_out.astype(bf16), gains(g_mix_post), gains(g_mlp_pre), w_up.astype(bf16), w_down.astype(bf16),
              gains(g_mlp_post))

    new_p = [[] for _ in range(5)]
    conv_s = []
    new_a = new_b = None
    for l in range(depth):
        cw = conv_w[l].astype(f32)
        sinks = b_sinks[l].astype(f32)

        (qa1, qa4, qa16, ka1, ka4, ka16, va1, va4, va16, qb, kbx, vbx, oc,
         kat, vat, kbt, vbt, ut) = _prompt_proj(l, xp, row(g_mix_pre[l]), w_in_b, cw)
        pats = [_band_attention(q, k, v, bias) for q, k, v, bias in
                ((qa1, ka1, va1, band_a[0]), (qa4, ka4, va4, band_a[1]), (qa16, ka16, va16, band_a[2]))]
        ob = _band_attention(qb, kbx, vbx, band_b, sinks=sinks)
        for lst, a in zip(new_p, (kat[None, None], vat[None, None], kbt[None, None], vbt[None, None],
                                  ut.reshape(1, CONV_WIDTH - 1, C_WIDTH))):
            lst.append(a)

        proj_s = _sample_proj(l, xs, row(g_mix_pre[l]), w_in_b)
        qkv_s = [a.reshape(n_dec, t_dec, A_WIDTH) for a in proj_s[:3]]
        xp, oa_s, new_a = _prompt_finish(l, xp, [p[0] for p in pats] + [p[1] for p in pats] + [ob, oc], qkv_s,
                                         expand, tab_sa, params, caches_a, new_a)

        mix_s, new_b, nst = _sample_rest(l, sinks, oa_s, proj_s[3:], caches_b, tab_sb, cw, new_b)
        conv_s.append(nst)
        xs = _sample_finish(l, xs, mix_s, params)

    a_k_p, a_v_p, b_k_p, b_v_p = [jnp.concatenate(t, axis=0) for t in new_p[:4]]
    return (xp.reshape(batch, seq, d_model), xs.reshape(n_dec, t_dec, d_model),
            from_fm(a_k_p, HEADS), from_fm(a_v_p, HEADS), from_fm(b_k_p, 2), from_fm(b_v_p, 2),
            jnp.stack(new_p[4], axis=0),
            from_fm(new_a[0], HEADS), from_fm(new_a[1], HEADS), from_fm(new_b[0], 2), from_fm(new_b[1], 2),
            jnp.stack(conv_s, axis=0))
```

```python
import functools
import math

import jax
import jax.numpy as jnp
from jax import lax
from jax.experimental import pallas as pl
from jax.experimental.pallas import tpu as pltpu

HEAD_DIM = 64
HEADS = 6
A_WIDTH = HEADS * HEAD_DIM
B_KV_WIDTH = 2 * HEAD_DIM
C_WIDTH = 256
A_PATTERNS = ((128, 1), (512, 4), (2048, 16))
A_WINDOW = 2048
B_WINDOW = 128
BLOCK = 128
CONV_WIDTH = 3
NUM_BUCKETS = 32
MAX_DISTANCE = 2048
EPS = 1e-6
SCALE = 1.0 / math.sqrt(HEAD_DIM)
NEG = -1e30
LANES = 128
SUBLANES = 8
ROW_TILE = 512
Q_TILE = 1024
GROUP = 4
FF_CHUNK = 512
SEQS_PER_STEP = 16
READ_AHEAD = 2
READ_SLOTS = READ_AHEAD + 1
WRITE_SLOTS = 2
VMEM_LIMIT = 56 * 1024 * 1024
VMEM_LIMIT_STREAMING = 60 * 1024 * 1024

_IN_SIZES = (A_WIDTH, A_WIDTH, A_WIDTH, A_WIDTH, B_KV_WIDTH, B_KV_WIDTH, C_WIDTH, C_WIDTH, C_WIDTH)
_IN_OFFS = tuple(sum(_IN_SIZES[:i]) for i in range(len(_IN_SIZES) + 1))

f32 = jnp.float32
bf16 = jnp.bfloat16


def _rms(x, g):
    ms = jnp.mean(x * x, axis=-1, keepdims=True)
    return (x * lax.rsqrt(ms + EPS)) * g


def _dot(a, b):
    return jnp.dot(a, b, preferred_element_type=f32)


def _dot_nt(a, b):
    return lax.dot_general(a, b, (((1,), (1,)), ((), ())), preferred_element_type=f32)


def _project(x_ref, g_ref, w_ref):
    hb = _rms(x_ref[...], g_ref[...]).astype(bf16)
    proj = _dot(hb, w_ref[...])
    return [proj[:, _IN_OFFS[i]:_IN_OFFS[i + 1]] for i in range(len(_IN_SIZES))]


def _swap_halves(t):
    return pltpu.roll(t, HEAD_DIM, axis=1)


def _low_half(shape):
    return lax.broadcasted_iota(jnp.int32, shape, 1) < HEAD_DIM


def _layer_block(a, layer):
    return pl.BlockSpec((None,) + a.shape[1:], lambda *_: (layer, 0, 0), pipeline_mode=pl.Buffered(1))


def _write_grouped(x, nat_ref, g4_ref, g16_ref, nat_s, g4_s):
    t, w = x.shape
    slabs = w // LANES
    nat_ref[...] = x.astype(bf16)
    for s in range(slabs):
        nat_s[s] = x[:, s * LANES:(s + 1) * LANES]
    for r in range(GROUP):
        for s in range(slabs):
            g = nat_s[s, pl.ds(r, t // GROUP, stride=GROUP), :]
            g4_s[r * slabs + s] = g
            g4_ref[:, (r * slabs + s) * LANES:(r * slabs + s + 1) * LANES] = g.astype(bf16)
    for c in range(GROUP * GROUP):
        r, r2 = c % GROUP, c // GROUP
        for s in range(slabs):
            h = g4_s[r * slabs + s, pl.ds(r2, t // (GROUP * GROUP), stride=GROUP), :]
            g16_ref[:, (c * slabs + s) * LANES:(c * slabs + s + 1) * LANES] = h.astype(bf16)


def _prompt_proj_kernel(n_steps, x_ref, g_ref, w_ref, cw_ref,
                        qa1_ref, qa4_ref, qa16_ref, ka1_ref, ka4_ref, ka16_ref, va1_ref, va4_ref, va16_ref,
                        qb_ref, kbx_ref, vbx_ref, oc_ref,
                        kat_ref, vat_ref, kbt_ref, vbt_ref, ut_ref, uext_ref, nat_s, g4_s):
    t = x_ref.shape[0]
    step = pl.program_id(0)
    aq, ak, av, bq, bk, bv, cb, cc, cx = _project(x_ref, g_ref, w_ref)
    _write_grouped(aq * SCALE, qa1_ref, qa4_ref, qa16_ref, nat_s, g4_s)
    _write_grouped(ak, ka1_ref, ka4_ref, ka16_ref, nat_s, g4_s)
    _write_grouped(av, va1_ref, va4_ref, va16_ref, nat_s, g4_s)
    qb_ref[...] = (bq * SCALE).astype(bf16)
    low = _low_half(bk.shape)

    def widen(kv):
        sw = _swap_halves(kv)
        return jnp.concatenate([jnp.where(low, kv, sw), kv, jnp.where(low, sw, kv)], axis=1)

    kbx_ref[...] = widen(bk).astype(bf16)
    vbx_ref[...] = widen(bv).astype(bf16)

    @pl.when(step >= n_steps - A_WINDOW // t)
    def _():
        kat_ref[...] = ak.T
        vat_ref[...] = av.T

    @pl.when(step == 0)
    def _():
        uext_ref[0:SUBLANES, :] = jnp.zeros((SUBLANES, C_WIDTH), f32)

    u = cc * cx
    uext_ref[SUBLANES:, :] = u
    u1 = uext_ref[pl.ds(SUBLANES - 1, t), :]
    u2 = uext_ref[pl.ds(SUBLANES - 2, t), :]
    cw = cw_ref[...]
    conv = cw[0:1, :] * u2 + cw[1:2, :] * u1 + cw[2:3, :] * u
    oc_ref[...] = (cb * conv).astype(bf16)

    @pl.when(step == n_steps - 1)
    def _():
        kbt_ref[...] = bk[t - B_WINDOW:, :].T
        vbt_ref[...] = bv[t - B_WINDOW:, :].T
        ut_ref[...] = uext_ref[pl.ds(t + SUBLANES - (CONV_WIDTH - 1), CONV_WIDTH - 1), :]

    uext_ref[0:SUBLANES, :] = u[t - SUBLANES:, :]


def _prompt_proj(layer, x, g, w, cw):
    s, d = x.shape
    t = ROW_TILE
    n = s // t
    tail_blocks = A_WINDOW // t
    row = lambda width: pl.BlockSpec((t, width), lambda i: (i, 0))
    const = lambda shape: pl.BlockSpec(shape, lambda i: (0, 0))
    tail = pl.BlockSpec((A_WIDTH, t), lambda i: (0, jnp.maximum(i - (n - tail_blocks), 0)))
    grouped_specs, grouped_shapes = [], []
    for _ in range(3):
        for _, dil in A_PATTERNS:
            grouped_specs.append(pl.BlockSpec((t // dil, dil * A_WIDTH), lambda i: (i, 0)))
            grouped_shapes.append(jax.ShapeDtypeStruct((s // dil, dil * A_WIDTH), bf16))
    bfo = lambda width: jax.ShapeDtypeStruct((s, width), bf16)
    return pl.pallas_call(
        functools.partial(_prompt_proj_kernel, n),
        grid=(n,),
        in_specs=[row(d), const((1, d)), _layer_block(w, layer), const(cw.shape)],
        out_specs=grouped_specs + [row(A_WIDTH)] * 3 + [row(C_WIDTH), tail, tail,
                                                        const((B_KV_WIDTH, B_WINDOW)), const((B_KV_WIDTH, B_WINDOW)),
                                                        const((CONV_WIDTH - 1, C_WIDTH))],
        out_shape=grouped_shapes + [bfo(A_WIDTH)] * 3 + [bfo(C_WIDTH),
                                                         jax.ShapeDtypeStruct((A_WIDTH, A_WINDOW), f32),
                                                         jax.ShapeDtypeStruct((A_WIDTH, A_WINDOW), f32),
                                                         jax.ShapeDtypeStruct((B_KV_WIDTH, B_WINDOW), f32),
                                                         jax.ShapeDtypeStruct((B_KV_WIDTH, B_WINDOW), f32),
                                                         jax.ShapeDtypeStruct((CONV_WIDTH - 1, C_WIDTH), f32)],
        scratch_shapes=[pltpu.VMEM((t + SUBLANES, C_WIDTH), f32),
                        pltpu.VMEM((A_WIDTH // LANES, t, LANES), f32),
                        pltpu.VMEM((GROUP * A_WIDTH // LANES, t // GROUP, LANES), f32)],
        compiler_params=pltpu.CompilerParams(dimension_semantics=("arbitrary",),
                                             vmem_limit_bytes=VMEM_LIMIT),
        name="prompt_proj",
    )(x, g, w, cw)


def _band_attn_kernel(gated, *refs):
    if gated:
        sink_ref, q_ref, kp_ref, kc_ref, vp_ref, vc_ref, bias_ref, o_ref = refs
    else:
        q_ref, kp_ref, kc_ref, vp_ref, vc_ref, bias_ref, o_ref, lse_ref = refs
    has_prev = jnp.minimum(pl.program_id(1), 1)
    kcat = jnp.concatenate([kp_ref[...], kc_ref[...]], axis=0)
    vcat = jnp.concatenate([vp_ref[...], vc_ref[...]], axis=0)
    low = _low_half((BLOCK, LANES))
    lane = lax.broadcasted_iota(jnp.int32, (BLOCK, LANES), 1)
    zero = jnp.zeros((BLOCK, LANES), bf16)
    for b in range(q_ref.shape[0] // BLOCK):
        rows = slice(b * BLOCK, (b + 1) * BLOCK)
        variant = has_prev if b == 0 else 1
        lse_tile = jnp.zeros((BLOCK, LANES), f32)
        scores = []
        for h in range(HEADS):
            cols = slice((h // 2) * LANES, (h // 2 + 1) * LANES)
            qm = jnp.where(low if h % 2 == 0 else ~low, q_ref[rows, cols], zero)
            scores.append(_dot_nt(qm, kcat[b * BLOCK:(b + 2) * BLOCK, cols]) + bias_ref[variant, h])
        soft = [_softmax_rows(s) for s in scores]
        outs = [_dot(pe.astype(bf16), vcat[b * BLOCK:(b + 2) * BLOCK, (h // 2) * LANES:(h // 2 + 1) * LANES])
                for h, (pe, _, _) in enumerate(soft)]
        for h, (_, m, l) in enumerate(soft):
            outs[h] = outs[h] / l
            lse = m + jnp.log(l)
            if gated:
                outs[h] = outs[h] * jax.nn.sigmoid(lse - sink_ref[h])
            else:
                lse_tile = jnp.where(lane == h, lse, lse_tile)
        for p in range(HEADS // 2):
            o_ref[rows, p * LANES:(p + 1) * LANES] = jnp.where(low, outs[2 * p], outs[2 * p + 1]).astype(bf16)
        if not gated:
            lse_ref[rows, :] = lse_tile


def _band_attention(q, k, v, bias, sinks=None):
    rows, width = q.shape
    d = width // A_WIDTH
    per_tile = Q_TILE // BLOCK
    cur = pl.BlockSpec((Q_TILE, A_WIDTH), lambda c, j: (j, c))
    prev = pl.BlockSpec((BLOCK, A_WIDTH), lambda c, j: (jnp.maximum(j * per_tile - 1, 0), c))
    bias_spec = pl.BlockSpec(bias.shape, lambda c, j: (0, 0, 0, 0))
    gated = sinks is not None
    in_specs = [cur, prev, cur, prev, cur, bias_spec]
    args = [q, k, k, v, v, bias]
    out_specs = [cur]
    out_shape = [jax.ShapeDtypeStruct((rows, width), bf16)]
    if gated:
        in_specs = [pl.BlockSpec(memory_space=pltpu.SMEM)] + in_specs
        args = [sinks] + args
    else:
        out_specs.append(pl.BlockSpec((Q_TILE, LANES), lambda c, j: (j, c)))
        out_shape.append(jax.ShapeDtypeStruct((rows, d * LANES), f32))
    outs = pl.pallas_call(
        functools.partial(_band_attn_kernel, gated),
        grid=(d, rows // Q_TILE),
        in_specs=in_specs, out_specs=out_specs, out_shape=out_shape,
        compiler_params=pltpu.CompilerParams(dimension_semantics=("arbitrary", "arbitrary"),
                                             vmem_limit_bytes=VMEM_LIMIT),
        name=f"band_attn_d{d}" + ("_gated" if gated else ""),
    )(*args)
    return outs[0] if gated else tuple(outs)


def _pad_rows(new, at_end):
    z = jnp.zeros((LANES - new.shape[0], new.shape[1]), f32)
    return jnp.concatenate([z, new] if at_end else [new, z], axis=0)


def _softmax_rows(s):
    m = jnp.max(s, axis=-1, keepdims=True)
    p = jnp.exp(s - m)
    l = jnp.sum(p, axis=-1, keepdims=True)
    return p, m, l


def _cache_scores(qbd, kt_ref, k_new, tab_ref):
    s = jnp.concatenate([_dot(qbd, kt_ref[...].astype(bf16)),
                         _dot_nt(qbd, _pad_rows(k_new, False).astype(bf16))], axis=1) + tab_ref[...]
    p, m, l = _softmax_rows(s)
    return p.astype(bf16), m, l


def _cache_values(pb, l, vt_ref, v_new):
    buf = vt_ref.shape[1]
    res = _dot_nt(pb[:, :buf], vt_ref[...].astype(bf16)) + _dot(pb[:, buf:], _pad_rows(v_new, False).astype(bf16))
    return res / l


def _shift_into(dst_ref, ct_ref, new):
    w, buf = ct_ref.shape
    t = new.shape[0]
    new_t = jnp.transpose(_pad_rows(new, True))
    rolled = pltpu.roll(ct_ref[...], buf - t, axis=1)
    if buf > LANES:
        dst_ref[:, 0:buf - LANES] = rolled[:, 0:buf - LANES]
    lane = lax.broadcasted_iota(jnp.int32, (w, LANES), 1)
    dst_ref[:, buf - LANES:] = jnp.where(lane >= LANES - t, new_t, rolled[:, buf - LANES:])


def _own_head_mask(t):
    rows = HEADS * t
    row_head = lax.broadcasted_iota(jnp.int32, (rows, A_WIDTH), 0) >> (t.bit_length() - 1)
    col_head = lax.broadcasted_iota(jnp.int32, (rows, A_WIDTH), 1) >> (HEAD_DIM.bit_length() - 1)
    return row_head == col_head


def _out_proj_stage(x, mix_b, wo_ref, gpost_ref, gpre2_ref):
    x1 = x + _rms(_dot(mix_b, wo_ref[...]), gpost_ref[...])
    return x1, _rms(x1, gpre2_ref[...]).astype(bf16)


def _ffn_up(hb, wup_ref, c):
    a = jnp.maximum(_dot(hb, wup_ref[:, c * FF_CHUNK:(c + 1) * FF_CHUNK]), 0.0)
    return (a * a).astype(bf16)


def _ffn_down(acc, a2, wdn_ref, c):
    return acc + _dot(a2, wdn_ref[c * FF_CHUNK:(c + 1) * FF_CHUNK, :])


def _ungroup(blk_ref, w, nat_s, g4_s):
    rows, width = blk_ref.shape
    slabs, d = w // LANES, width // w
    if d == 1:
        return blk_ref[...].astype(f32)
    if d == GROUP * GROUP:
        for c in range(d):
            r, r2 = c % GROUP, c // GROUP
            for s in range(slabs):
                col = (c * slabs + s) * LANES
                g4_s[r * slabs + s, pl.ds(r2, rows, stride=GROUP), :] = blk_ref[:, col:col + LANES].astype(f32)
        for r in range(GROUP):
            for s in range(slabs):
                nat_s[s, pl.ds(r, rows * GROUP, stride=GROUP), :] = g4_s[r * slabs + s]
    else:
        assert d == GROUP
        for r in range(GROUP):
            for s in range(slabs):
                col = (r * slabs + s) * LANES
                nat_s[s, pl.ds(r, rows, stride=GROUP), :] = blk_ref[:, col:col + LANES].astype(f32)
    return jnp.concatenate([nat_s[s] for s in range(slabs)], axis=1)


def _prompt_finish_kernel(layer, n_alias, x_ref, o1_ref, o4_ref, o16_ref, l1_ref, l4_ref, l16_ref, ob_ref, oc_ref,
                          qa_ref, ka_ref, va_ref, exp_ref, taba_ref, wo_ref, gpost_ref, gpre2_ref, wup_ref, wdn_ref,
                          gpost2_ref, cak_hbm, cav_hbm, *rest):
    out_ref, oa_ref, nak_hbm, nav_hbm, nat_s, g4_s, in_buf, out_buf, in_sem, out_sem = rest[n_alias:]
    step, n_steps = pl.program_id(0), pl.num_programs(0)
    nseq, t = qa_ref.shape[0], qa_ref.shape[1]
    n_units = 2 * nseq
    own = _own_head_mask(t)
    st = {}
    base = step * nseq
    first_slot = lax.rem(step * n_units, READ_SLOTS)

    def read_slot(u):
        return lax.rem(first_slot + u, READ_SLOTS)

    def read(u):
        slot = read_slot(u)
        src = (cak_hbm, cav_hbm)[u % 2].at[layer, base + u // 2]
        return pltpu.make_async_copy(src, in_buf.at[slot], in_sem.at[slot])

    def write(u):
        dst = (nak_hbm, nav_hbm)[u % 2].at[layer, base + u // 2]
        return pltpu.make_async_copy(out_buf.at[u % 2], dst, out_sem.at[u % 2])

    def merge_stage():
        lses = [_ungroup(l_ref, LANES, nat_s, g4_s) for l_ref in (l1_ref, l4_ref, l16_ref)]
        m = jnp.maximum(jnp.maximum(lses[0], lses[1]), lses[2])
        es = [jnp.exp(l - m) for l in lses]
        tot = es[0] + es[1] + es[2]
        oa = jnp.zeros((x_ref.shape[0], A_WIDTH), f32)
        for e, o_ref in zip(es, (o1_ref, o4_ref, o16_ref)):
            w = e / tot
            w_hi = w.astype(bf16)
            w_lo = (w - w_hi.astype(f32)).astype(bf16)
            wx = _dot(jnp.concatenate([w_hi, w_lo], axis=1), exp_ref[...])
            oa = oa + wx * _ungroup(o_ref, A_WIDTH, nat_s, g4_s)
        st["mix"] = jnp.concatenate([oa.astype(bf16), ob_ref[...], oc_ref[...]], axis=1)

    def proj_stage():
        st["x1"], st["hb"] = _out_proj_stage(x_ref[...], st["mix"], wo_ref, gpost_ref, gpre2_ref)
        st["acc"] = jnp.zeros(x_ref.shape, f32)

    def up_stage(c):
        st["a", c] = _ffn_up(st["hb"], wup_ref, c)

    def down_stage(c):
        st["acc"] = _ffn_down(st["acc"], st.pop(("a", c)), wdn_ref, c)

    def final_stage():
        out_ref[...] = st["x1"] + _rms(st["acc"], gpost2_ref[...])

    n_chunks = wup_ref.shape[1] // FF_CHUNK
    pairs = [[functools.partial(up_stage, c), functools.partial(down_stage, c - 1)] for c in range(1, n_chunks)]
    share, extra = divmod(len(pairs), n_units - 1)
    stages = [[merge_stage, proj_stage, functools.partial(up_stage, 0)]]
    for u in range(n_units - 1):
        stages.append([f for _ in range(share + (1 if u < extra else 0)) for f in pairs.pop(0)])
    stages[-1] += [functools.partial(down_stage, n_chunks - 1), final_stage]

    def keys_unit(s, src_ref):
        q6 = jnp.concatenate([qa_ref[s]] * HEADS, axis=0)
        qbd = jnp.where(own, q6, 0.0).astype(bf16)
        st["p"], _, st["l"] = _cache_scores(qbd, src_ref, ka_ref[s], taba_ref)
        _shift_into(out_buf.at[0], src_ref, ka_ref[s])

    def values_unit(s, src_ref):
        res = jnp.where(own, _cache_values(st["p"], st["l"], src_ref, va_ref[s]), 0.0)
        oa = res[0:t, :]
        for h in range(1, HEADS):
            oa = oa + res[h * t:(h + 1) * t, :]
        oa_ref[s] = oa
        _shift_into(out_buf.at[1], src_ref, va_ref[s])

    @pl.when(step == 0)
    def _():
        for u in range(READ_AHEAD):
            read(u).start()

    for u in range(n_units):
        if u + READ_AHEAD < n_units:
            read(u + READ_AHEAD).start()
        else:
            @pl.when(step + 1 < n_steps)
            def _():
                read(u + READ_AHEAD).start()
        read(u).wait()
        if u >= 2:
            write(u - 2).wait()
        else:
            @pl.when(step > 0)
            def _():
                write(u).wait()
        for stage in stages[u]:
            stage()
        (keys_unit if u % 2 == 0 else values_unit)(u // 2, in_buf.at[read_slot(u)])
        write(u).start()

    @pl.when(step == n_steps - 1)
    def _():
        write(n_units - 2).wait()
        write(n_units - 1).wait()


def _prompt_finish(layer, x, row_inputs, sample_qkv, expand, taba, params, caches, prev_outs):
    s, d = x.shape
    t = ROW_TILE
    n_steps = s // t
    n_seq = sample_qkv[0].shape[0]
    assert n_seq % n_steps == 0
    per_step = n_seq // n_steps
    row = lambda a: pl.BlockSpec((a.shape[0] * t // s, a.shape[1]), lambda i: (i, 0))
    seqs = lambda a: pl.BlockSpec((per_step,) + a.shape[1:], lambda i: (i, 0, 0))
    const = lambda a: pl.BlockSpec(a.shape, lambda i: (0, 0), pipeline_mode=pl.Buffered(1))
    gain = lambda a: pl.BlockSpec((None, 1, a.shape[2]), lambda i: (layer, 0, 0))
    w_out, g_post, g_pre2, w_up, w_down, g_post2 = params
    cak, cav = caches
    n_alias = 0 if prev_outs is None else len(prev_outs)
    any_spec = pl.BlockSpec(memory_space=pl.ANY)
    args = ([x] + row_inputs + list(sample_qkv)
            + [expand, taba, w_out, g_post, g_pre2, w_up, w_down, g_post2, cak, cav]
            + (list(prev_outs) if n_alias else []))
    in_specs = ([row(a) for a in [x] + row_inputs] + [seqs(a) for a in sample_qkv]
                + [const(expand), const(taba), _layer_block(w_out, layer), gain(g_post), gain(g_pre2),
                   _layer_block(w_up, layer), _layer_block(w_down, layer), gain(g_post2), any_spec, any_spec]
                + [any_spec] * n_alias)
    first_alias = len(args) - n_alias
    assert (2 * per_step) % WRITE_SLOTS == 0
    outs = pl.pallas_call(
        functools.partial(_prompt_finish_kernel, layer, n_alias),
        grid=(n_steps,),
        in_specs=in_specs,
        out_specs=[pl.BlockSpec((t, d), lambda i: (i, 0)), seqs(sample_qkv[0]), any_spec, any_spec],
        out_shape=[jax.ShapeDtypeStruct((s, d), f32), jax.ShapeDtypeStruct(sample_qkv[0].shape, f32),
                   jax.ShapeDtypeStruct(cak.shape, f32), jax.ShapeDtypeStruct(cav.shape, f32)],
        scratch_shapes=[pltpu.VMEM((A_WIDTH // LANES, t, LANES), f32),
                        pltpu.VMEM((GROUP * A_WIDTH // LANES, t // GROUP, LANES), f32),
                        pltpu.VMEM((READ_SLOTS,) + cak.shape[2:], f32), pltpu.VMEM((WRITE_SLOTS,) + cak.shape[2:], f32),
                        pltpu.SemaphoreType.DMA((READ_SLOTS,)), pltpu.SemaphoreType.DMA((WRITE_SLOTS,))],
        input_output_aliases={first_alias + i: 2 + i for i in range(n_alias)},
        compiler_params=pltpu.CompilerParams(dimension_semantics=("arbitrary",),
                                             vmem_limit_bytes=VMEM_LIMIT_STREAMING),
        name=f"prompt_finish_l{layer}",
    )(*args)
    return outs[0], outs[1], (outs[2], outs[3])


def _sample_finish_kernel(x_ref, mix_ref, wo_ref, gpost_ref, gpre2_ref, wup_ref, wdn_ref, gpost2_ref, out_ref):
    x1, hb = _out_proj_stage(x_ref[...], mix_ref[...].astype(bf16), wo_ref, gpost_ref, gpre2_ref)
    acc = jnp.zeros(x1.shape, f32)
    n_chunks = wup_ref.shape[1] // FF_CHUNK
    a2 = _ffn_up(hb, wup_ref, 0)
    for c in range(n_chunks):
        a2_next = _ffn_up(hb, wup_ref, c + 1) if c + 1 < n_chunks else None
        acc = _ffn_down(acc, a2, wdn_ref, c)
        a2 = a2_next
    out_ref[...] = x1 + _rms(acc, gpost2_ref[...])


def _sample_finish(layer, x, mix, params):
    s, d = x.shape
    t = ROW_TILE
    row = lambda a: pl.BlockSpec((t, a.shape[1]), lambda i: (i, 0))
    gain = lambda a: pl.BlockSpec((None, 1, a.shape[2]), lambda i: (layer, 0, 0))
    w_out, g_post, g_pre2, w_up, w_down, g_post2 = params
    return pl.pallas_call(
        _sample_finish_kernel,
        grid=(s // t,),
        in_specs=[row(x), row(mix), _layer_block(w_out, layer), gain(g_post), gain(g_pre2),
                  _layer_block(w_up, layer), _layer_block(w_down, layer), gain(g_post2)],
        out_specs=pl.BlockSpec((t, d), lambda i: (i, 0)),
        out_shape=jax.ShapeDtypeStruct((s, d), f32),
        compiler_params=pltpu.CompilerParams(dimension_semantics=("arbitrary",),
                                             vmem_limit_bytes=VMEM_LIMIT),
        name="sample_finish",
    )(x, mix, w_out, g_post, g_pre2, w_up, w_down, g_post2)


def _sample_proj_kernel(x_ref, g_ref, w_ref, qa_ref, ka_ref, va_ref, qb_ref, kb_ref, vb_ref, cb_ref, u_ref):
    aq, ak, av, bq, bk, bv, cb, cc, cx = _project(x_ref, g_ref, w_ref)
    qa_ref[...] = aq * SCALE
    ka_ref[...] = ak
    va_ref[...] = av
    qb_ref[...] = bq * SCALE
    kb_ref[...] = bk
    vb_ref[...] = bv
    cb_ref[...] = cb
    u_ref[...] = cc * cx


def _sample_proj(layer, x, g, w):
    s, d = x.shape
    t = ROW_TILE
    widths = (A_WIDTH, A_WIDTH, A_WIDTH, A_WIDTH, B_KV_WIDTH, B_KV_WIDTH, C_WIDTH, C_WIDTH)
    row = lambda width: pl.BlockSpec((t, width), lambda i: (i, 0))
    const = lambda shape: pl.BlockSpec(shape, lambda i: (0, 0))
    return pl.pallas_call(
        _sample_proj_kernel,
        grid=(s // t,),
        in_specs=[row(d), const((1, d)), _layer_block(w, layer)],
        out_specs=[row(wd) for wd in widths],
        out_shape=[jax.ShapeDtypeStruct((s, wd), f32) for wd in widths],
        compiler_params=pltpu.CompilerParams(dimension_semantics=("arbitrary",),
                                             vmem_limit_bytes=VMEM_LIMIT),
        name="sample_proj",
    )(x, g, w)


def _sample_rest_kernel(n_alias, sink_ref, oa_ref, qb_ref, kb_ref, vb_ref, cb_ref, u_ref,
                        cbk_ref, cbv_ref, cst_ref, tabb_ref, cw_ref, *rest):
    mix_ref, nbk_ref, nbv_ref, nst_ref, uext_s = rest[n_alias:]
    t = qb_ref.shape[1]
    rows = HEADS * t
    low = _low_half((t, LANES))
    zero = jnp.zeros((t, LANES), f32)
    row_head = lax.broadcasted_iota(jnp.int32, (rows, 1), 0) >> (t.bit_length() - 1)
    sink_col = jnp.zeros((rows, 1), f32)
    for h in range(HEADS):
        sink_col = jnp.where(row_head == h, sink_ref[h], sink_col)
    cw = cw_ref[...]
    n_seq = qb_ref.shape[0]
    scores = []
    for g in range(n_seq):
        qb = qb_ref[g]
        t0, t1, t2 = (qb[:, i * LANES:(i + 1) * LANES] for i in range(3))
        qbd = jnp.concatenate([
            jnp.where(low, t0, zero), jnp.where(low, _swap_halves(t0), zero), jnp.where(low, t1, zero),
            jnp.where(low, zero, t1), jnp.where(low, zero, _swap_halves(t2)), jnp.where(low, zero, t2)],
            axis=0).astype(bf16)
        scores.append(_cache_scores(qbd, cbk_ref.at[g], kb_ref[g], tabb_ref))
    outs = [_cache_values(pb, lb, cbv_ref.at[g], vb_ref[g]) for g, (pb, _, lb) in enumerate(scores)]
    for g in range(n_seq):
        _, mb, lb = scores[g]
        resb = outs[g] * jax.nn.sigmoid(mb + jnp.log(lb) - sink_col)
        r = [resb[h * t:(h + 1) * t, :] for h in range(HEADS)]
        ob = jnp.concatenate([jnp.where(low, r[0], _swap_halves(r[1])), jnp.where(low, r[2], r[3]),
                              jnp.where(low, _swap_halves(r[4]), r[5])], axis=1)
        _shift_into(nbk_ref.at[g], cbk_ref.at[g], kb_ref[g])
        _shift_into(nbv_ref.at[g], cbv_ref.at[g], vb_ref[g])

        u = u_ref[g]
        uext_s[g, SUBLANES - (CONV_WIDTH - 1):SUBLANES, :] = cst_ref[g]
        uext_s[g, SUBLANES:, :] = u
        u1 = uext_s[g, pl.ds(SUBLANES - 1, t), :]
        u2 = uext_s[g, pl.ds(SUBLANES - 2, t), :]
        oc = cb_ref[g] * (cw[0:1, :] * u2 + cw[1:2, :] * u1 + cw[2:3, :] * u)
        nst_ref[g] = uext_s[g, pl.ds(SUBLANES + t - (CONV_WIDTH - 1), CONV_WIDTH - 1), :]
        mix_ref[g] = jnp.concatenate([oa_ref[g], ob, oc], axis=1)


def _sample_rest(layer, sinks, oa, proj, caches, tabb, cw, prev_outs):
    cbk, cbv, cst = caches
    n, t = oa.shape[0], oa.shape[1]
    g = SEQS_PER_STEP
    proj3 = [a.reshape(n, t, a.shape[1]) for a in proj]
    per_seq = lambda a: pl.BlockSpec((g,) + a.shape[1:], lambda i: (i, 0, 0))
    per_layer_seq = lambda a: pl.BlockSpec((None, g) + a.shape[2:], lambda i: (layer, i, 0, 0))
    const = lambda a: pl.BlockSpec(a.shape, lambda i: (0, 0))
    n_alias = 0 if prev_outs is None else len(prev_outs)
    args = [sinks, oa] + proj3 + [cbk, cbv, cst, tabb, cw] + (list(prev_outs) if n_alias else [])
    in_specs = ([pl.BlockSpec(memory_space=pltpu.SMEM), per_seq(oa)] + [per_seq(a) for a in proj3]
                + [per_layer_seq(a) for a in caches] + [const(tabb), const(cw)]
                + [pl.BlockSpec(memory_space=pl.ANY)] * n_alias)
    first_alias = len(args) - n_alias
    outs = pl.pallas_call(
        functools.partial(_sample_rest_kernel, n_alias),
        grid=(n // g,),
        in_specs=in_specs,
        out_specs=[pl.BlockSpec((g, t, 4 * C_WIDTH), lambda i: (i, 0, 0)), per_layer_seq(cbk), per_layer_seq(cbv),
                   pl.BlockSpec((g, CONV_WIDTH - 1, C_WIDTH), lambda i: (i, 0, 0))],
        out_shape=[jax.ShapeDtypeStruct((n, t, 4 * C_WIDTH), f32), jax.ShapeDtypeStruct(cbk.shape, f32),
                   jax.ShapeDtypeStruct(cbv.shape, f32), jax.ShapeDtypeStruct((n, CONV_WIDTH - 1, C_WIDTH), f32)],
        scratch_shapes=[pltpu.VMEM((g, SUBLANES + t, C_WIDTH), f32)],
        input_output_aliases={first_alias + i: 1 + i for i in range(n_alias)},
        compiler_params=pltpu.CompilerParams(dimension_semantics=("arbitrary",),
                                             vmem_limit_bytes=VMEM_LIMIT),
        name=f"sample_rest_l{layer}",
    )(*args)
    mix, nbk, nbv, nst = outs
    return mix.reshape(n * t, 4 * C_WIDTH), (nbk, nbv), nst


def _t5_bucket(dist):
    dist = jnp.maximum(dist, 0)
    max_exact = NUM_BUCKETS // 2
    scaled = jnp.log(jnp.maximum(dist, 1).astype(f32) / max_exact) / math.log(MAX_DISTANCE / max_exact)
    large = max_exact + (scaled * (NUM_BUCKETS - max_exact)).astype(jnp.int32)
    large = jnp.minimum(large, NUM_BUCKETS - 1)
    return jnp.where(dist < max_exact, dist, large)


def _bias_of(table, dist):
    onehot = jax.nn.one_hot(_t5_bucket(dist), NUM_BUCKETS, dtype=f32)
    return jnp.einsum('...b,bh->h...', onehot, table.astype(f32), precision=lax.Precision.HIGHEST)


def _band_bias(table, d):
    qi = jnp.arange(BLOCK)[:, None]
    kj = jnp.arange(2 * BLOCK)[None, :]
    delta = BLOCK + qi - kj
    in_band = (delta >= 0) & (delta <= BLOCK)
    bias = _bias_of(table, delta * d)
    with_prev = jnp.where(in_band[None], bias, NEG)
    no_prev = jnp.where((in_band & (kj >= BLOCK))[None], bias, NEG)
    return jnp.stack([no_prev, with_prev], axis=0)


def _sample_table_a(table, t, buf, cols):
    i = jnp.arange(t)[:, None]
    r = jnp.arange(cols)[None, :]
    delta = buf + i - r
    count = sum(((delta >= 0) & (delta % d == 0) & (delta // d <= w // d)).astype(f32) for w, d in A_PATTERNS)
    tab = jnp.where((count > 0)[None], _bias_of(table, delta) + jnp.log(jnp.maximum(count, 1.0))[None], NEG)
    return tab.reshape(table.shape[1] * t, cols)


def _sample_table_b(table, t, buf, cols):
    i = jnp.arange(t)[:, None]
    r = jnp.arange(cols)[None, :]
    delta = buf + i - r
    valid = (delta >= 0) & (delta <= B_WINDOW)
    return jnp.where(valid[None], _bias_of(table, delta), NEG).reshape(table.shape[1] * t, cols)


def _expand_matrix():
    lane = jnp.arange(LANES)[:, None]
    col_head = jnp.arange(A_WIDTH)[None, :] // HEAD_DIM
    e = (lane == col_head).astype(bf16)
    return jnp.concatenate([e, e], axis=0)


def kernel(x_prompt, x_sample, cache_a_k, cache_a_v, cache_b_k, cache_b_v, state_conv, w_in, w_out, conv_w,
           b_sinks, rel_bias, g_mix_pre, g_mix_post, w_up, w_down, g_mlp_pre, g_mlp_post):
    depth = w_in.shape[0]
    batch, seq, d_model = x_prompt.shape
    n_dec, t_dec, _ = x_sample.shape
    buf_a, buf_b = cache_a_k.shape[2], cache_b_k.shape[2]
    assert batch == 1 and seq % (A_PATTERNS[-1][1] * BLOCK) == 0 and seq >= A_WINDOW
    assert buf_a == A_WINDOW and buf_b == B_WINDOW and t_dec == SUBLANES
    assert (n_dec * t_dec) % ROW_TILE == 0 and seq % ROW_TILE == 0 and n_dec % SEQS_PER_STEP == 0
    assert tuple(d for _, d in A_PATTERNS) == (1, GROUP, GROUP * GROUP) and seq % (GROUP * GROUP * Q_TILE) == 0

    xp = x_prompt.reshape(seq, d_model)
    xs = x_sample.reshape(n_dec * t_dec, d_model)
    to_fm = lambda c: jnp.transpose(c, (0, 1, 3, 4, 2)).reshape(depth, n_dec, c.shape[3] * HEAD_DIM, c.shape[2])
    from_fm = lambda c, heads: jnp.transpose(c.reshape(c.shape[:2] + (heads, HEAD_DIM, c.shape[3])), (0, 1, 4, 2, 3))
    caches_a = (to_fm(cache_a_k), to_fm(cache_a_v))
    caches_b = (to_fm(cache_b_k), to_fm(cache_b_v), state_conv)
    table_a, table_b = rel_bias[:, :HEADS], rel_bias[:, HEADS:]
    band_a = [_band_bias(table_a, d) for _, d in A_PATTERNS]
    band_b = _band_bias(table_b, 1)
    tab_sa = _sample_table_a(table_a, t_dec, buf_a, buf_a + LANES)
    tab_sb = _sample_table_b(table_b, t_dec, buf_b, buf_b + LANES)
    expand = _expand_matrix()
    row = lambda v: v.reshape(1, -1).astype(f32)
    gains = lambda v: v.reshape(depth, 1, -1).astype(f32)
    w_in_b = w_in.astype(bf16)
    params = (w_out.astype(bf16), gains(g_mix_post), gains(g_mlp_pre), w_up.astype(bf16), w_down.astype(bf16),
              gains(g_mlp_post))

    new_p = [[] for _ in range(5)]
    conv_s = []
    new_a = new_b = None
    for l in range(depth):
        cw = conv_w[l].astype(f32)
        sinks = b_sinks[l].astype(f32)

        (qa1, qa4, qa16, ka1, ka4, ka16, va1, va4, va16, qb, kbx, vbx, oc,
         kat, vat, kbt, vbt, ut) = _prompt_proj(l, xp, row(g_mix_pre[l]), w_in_b, cw)
        pats = [_band_attention(q, k, v, bias) for q, k, v, bias in
                ((qa1, ka1, va1, band_a[0]), (qa4, ka4, va4, band_a[1]), (qa16, ka16, va16, band_a[2]))]
        ob = _band_attention(qb, kbx, vbx, band_b, sinks=sinks)
        for lst, a in zip(new_p, (kat[None, None], vat[None, None], kbt[None, None], vbt[None, None],
                                  ut.reshape(1, CONV_WIDTH - 1, C_WIDTH))):
            lst.append(a)

        proj_s = _sample_proj(l, xs, row(g_mix_pre[l]), w_in_b)
        qkv_s = [a.reshape(n_dec, t_dec, A_WIDTH) for a in proj_s[:3]]
        xp, oa_s, new_a = _prompt_finish(l, xp, [p[0] for p in pats] + [p[1] for p in pats] + [ob, oc], qkv_s,
                                         expand, tab_sa, params, caches_a, new_a)

        mix_s, new_b, nst = _sample_rest(l, sinks, oa_s, proj_s[3:], caches_b, tab_sb, cw, new_b)
        conv_s.append(nst)
        xs = _sample_finish(l, xs, mix_s, params)

    a_k_p, a_v_p, b_k_p, b_v_p = [jnp.concatenate(t, axis=0) for t in new_p[:4]]
    return (xp.reshape(batch, seq, d_model), xs.reshape(n_dec, t_dec, d_model),
            from_fm(a_k_p, HEADS), from_fm(a_v_p, HEADS), from_fm(b_k_p, 2), from_fm(b_v_p, 2),
            jnp.stack(new_p[4], axis=0),
            from_fm(new_a[0], HEADS), from_fm(new_a[1], HEADS), from_fm(new_b[0], 2), from_fm(new_b[1], 2),
            jnp.stack(conv_s, axis=0))
```

```python
import functools
import math

import jax
import jax.numpy as jnp
from jax import lax
from jax.experimental import pallas as pl
from jax.experimental.pallas import tpu as pltpu

HEAD_DIM = 64
HEADS = 6
A_WIDTH = HEADS * HEAD_DIM
B_KV_WIDTH = 2 * HEAD_DIM
C_WIDTH = 256
A_PATTERNS = ((128, 1), (512, 4), (2048, 16))
A_WINDOW = 2048
B_WINDOW = 128
BLOCK = 128
CONV_WIDTH = 3
NUM_BUCKETS = 32
MAX_DISTANCE = 2048
EPS = 1e-6
SCALE = 1.0 / math.sqrt(HEAD_DIM)
NEG = -1e30
LANES = 128
SUBLANES = 8
ROW_TILE = 512
Q_TILE = 1024
GROUP = 4
FF_CHUNK = 512
SEQS_PER_STEP = 16
READ_AHEAD = 3
READ_SLOTS = READ_AHEAD + 1
WRITE_SLOTS = 2
VMEM_LIMIT = 56 * 1024 * 1024
VMEM_LIMIT_STREAMING = 63 * 1024 * 1024

_IN_SIZES = (A_WIDTH, A_WIDTH, A_WIDTH, A_WIDTH, B_KV_WIDTH, B_KV_WIDTH, C_WIDTH, C_WIDTH, C_WIDTH)
_IN_OFFS = tuple(sum(_IN_SIZES[:i]) for i in range(len(_IN_SIZES) + 1))

f32 = jnp.float32
bf16 = jnp.bfloat16


def _rms(x, g):
    ms = jnp.mean(x * x, axis=-1, keepdims=True)
    return (x * lax.rsqrt(ms + EPS)) * g


def _dot(a, b):
    return jnp.dot(a, b, preferred_element_type=f32)


def _dot_nt(a, b):
    return lax.dot_general(a, b, (((1,), (1,)), ((), ())), preferred_element_type=f32)


def _project(x_ref, g_ref, w_ref):
    hb = _rms(x_ref[...], g_ref[...]).astype(bf16)
    proj = _dot(hb, w_ref[...])
    return [proj[:, _IN_OFFS[i]:_IN_OFFS[i + 1]] for i in range(len(_IN_SIZES))]


def _swap_halves(t):
    return pltpu.roll(t, HEAD_DIM, axis=1)


def _low_half(shape):
    return lax.broadcasted_iota(jnp.int32, shape, 1) < HEAD_DIM


def _layer_block(a, layer):
    return pl.BlockSpec((None,) + a.shape[1:], lambda *_: (layer, 0, 0), pipeline_mode=pl.Buffered(1))


def _write_grouped(x, nat_ref, g4_ref, g16_ref, nat_s, g4_s):
    t, w = x.shape
    slabs = w // LANES
    nat_ref[...] = x.astype(bf16)
    for s in range(slabs):
        nat_s[s] = x[:, s * LANES:(s + 1) * LANES]
    for r in range(GROUP):
        for s in range(slabs):
            g = nat_s[s, pl.ds(r, t // GROUP, stride=GROUP), :]
            g4_s[r * slabs + s] = g
            g4_ref[:, (r * slabs + s) * LANES:(r * slabs + s + 1) * LANES] = g.astype(bf16)
    for c in range(GROUP * GROUP):
        r, r2 = c % GROUP, c // GROUP
        for s in range(slabs):
            h = g4_s[r * slabs + s, pl.ds(r2, t // (GROUP * GROUP), stride=GROUP), :]
            g16_ref[:, (c * slabs + s) * LANES:(c * slabs + s + 1) * LANES] = h.astype(bf16)


def _prompt_proj_kernel(n_steps, x_ref, g_ref, w_ref, cw_ref,
                        qa1_ref, qa4_ref, qa16_ref, ka1_ref, ka4_ref, ka16_ref, va1_ref, va4_ref, va16_ref,
                        qb_ref, kbx_ref, vbx_ref, oc_ref,
                        kat_ref, vat_ref, kbt_ref, vbt_ref, ut_ref, uext_ref, nat_s, g4_s):
    t = x_ref.shape[0]
    step = pl.program_id(0)
    aq, ak, av, bq, bk, bv, cb, cc, cx = _project(x_ref, g_ref, w_ref)
    _write_grouped(aq * SCALE, qa1_ref, qa4_ref, qa16_ref, nat_s, g4_s)
    _write_grouped(ak, ka1_ref, ka4_ref, ka16_ref, nat_s, g4_s)
    _write_grouped(av, va1_ref, va4_ref, va16_ref, nat_s, g4_s)
    qb_ref[...] = (bq * SCALE).astype(bf16)
    low = _low_half(bk.shape)

    def widen(kv):
        sw = _swap_halves(kv)
        return jnp.concatenate([jnp.where(low, kv, sw), kv, jnp.where(low, sw, kv)], axis=1)

    kbx_ref[...] = widen(bk).astype(bf16)
    vbx_ref[...] = widen(bv).astype(bf16)

    @pl.when(step >= n_steps - A_WINDOW // t)
    def _():
        kat_ref[...] = ak.T
        vat_ref[...] = av.T

    @pl.when(step == 0)
    def _():
        uext_ref[0:SUBLANES, :] = jnp.zeros((SUBLANES, C_WIDTH), f32)

    u = cc * cx
    uext_ref[SUBLANES:, :] = u
    u1 = uext_ref[pl.ds(SUBLANES - 1, t), :]
    u2 = uext_ref[pl.ds(SUBLANES - 2, t), :]
    cw = cw_ref[...]
    conv = cw[0:1, :] * u2 + cw[1:2, :] * u1 + cw[2:3, :] * u
    oc_ref[...] = (cb * conv).astype(bf16)

    @pl.when(step == n_steps - 1)
    def _():
        kbt_ref[...] = bk[t - B_WINDOW:, :].T
        vbt_ref[...] = bv[t - B_WINDOW:, :].T
        ut_ref[...] = uext_ref[pl.ds(t + SUBLANES - (CONV_WIDTH - 1), CONV_WIDTH - 1), :]

    uext_ref[0:SUBLANES, :] = u[t - SUBLANES:, :]


def _prompt_proj(layer, x, g, w, cw):
    s, d = x.shape
    t = ROW_TILE
    n = s // t
    tail_blocks = A_WINDOW // t
    row = lambda width: pl.BlockSpec((t, width), lambda i: (i, 0))
    const = lambda shape: pl.BlockSpec(shape, lambda i: (0, 0))
    tail = pl.BlockSpec((A_WIDTH, t), lambda i: (0, jnp.maximum(i - (n - tail_blocks), 0)))
    grouped_specs, grouped_shapes = [], []
    for _ in range(3):
        for _, dil in A_PATTERNS:
            grouped_specs.append(pl.BlockSpec((t // dil, dil * A_WIDTH), lambda i: (i, 0)))
            grouped_shapes.append(jax.ShapeDtypeStruct((s // dil, dil * A_WIDTH), bf16))
    bfo = lambda width: jax.ShapeDtypeStruct((s, width), bf16)
    return pl.pallas_call(
        functools.partial(_prompt_proj_kernel, n),
        grid=(n,),
        in_specs=[row(d), const((1, d)), _layer_block(w, layer), const(cw.shape)],
        out_specs=grouped_specs + [row(A_WIDTH)] * 3 + [row(C_WIDTH), tail, tail,
                                                        const((B_KV_WIDTH, B_WINDOW)), const((B_KV_WIDTH, B_WINDOW)),
                                                        const((CONV_WIDTH - 1, C_WIDTH))],
        out_shape=grouped_shapes + [bfo(A_WIDTH)] * 3 + [bfo(C_WIDTH),
                                                         jax.ShapeDtypeStruct((A_WIDTH, A_WINDOW), f32),
                                                         jax.ShapeDtypeStruct((A_WIDTH, A_WINDOW), f32),
                                                         jax.ShapeDtypeStruct((B_KV_WIDTH, B_WINDOW), f32),
                                                         jax.ShapeDtypeStruct((B_KV_WIDTH, B_WINDOW), f32),
                                                         jax.ShapeDtypeStruct((CONV_WIDTH - 1, C_WIDTH), f32)],
        scratch_shapes=[pltpu.VMEM((t + SUBLANES, C_WIDTH), f32),
                        pltpu.VMEM((A_WIDTH // LANES, t, LANES), f32),
                        pltpu.VMEM((GROUP * A_WIDTH // LANES, t // GROUP, LANES), f32)],
        compiler_params=pltpu.CompilerParams(dimension_semantics=("arbitrary",),
                                             vmem_limit_bytes=VMEM_LIMIT),
        name="prompt_proj",
    )(x, g, w, cw)


def _band_attn_kernel(gated, *refs):
    if gated:
        sink_ref, q_ref, kp_ref, kc_ref, vp_ref, vc_ref, bias_ref, o_ref = refs
    else:
        q_ref, kp_ref, kc_ref, vp_ref, vc_ref, bias_ref, o_ref, lse_ref = refs
    has_prev = jnp.minimum(pl.program_id(1), 1)
    kcat = jnp.concatenate([kp_ref[...], kc_ref[...]], axis=0)
    vcat = jnp.concatenate([vp_ref[...], vc_ref[...]], axis=0)
    low = _low_half((BLOCK, LANES))
    lane = lax.broadcasted_iota(jnp.int32, (BLOCK, LANES), 1)
    zero = jnp.zeros((BLOCK, LANES), bf16)
    for b in range(q_ref.shape[0] // BLOCK):
        rows = slice(b * BLOCK, (b + 1) * BLOCK)
        variant = has_prev if b == 0 else 1
        lse_tile = jnp.zeros((BLOCK, LANES), f32)
        for p in range(HEADS // 2):
            cols = slice(p * LANES, (p + 1) * LANES)
            qt = q_ref[rows, cols]
            kt = kcat[b * BLOCK:(b + 2) * BLOCK, cols]
            vt = vcat[b * BLOCK:(b + 2) * BLOCK, cols]
            halves = []
            for e in range(2):
                h = 2 * p + e
                qm = jnp.where(low if e == 0 else ~low, qt, zero)
                s = _dot_nt(qm, kt) + bias_ref[variant, h]
                m = jnp.max(s, axis=-1, keepdims=True)
                pe = jnp.exp(s - m)
                l = jnp.sum(pe, axis=-1, keepdims=True)
                o = _dot(pe.astype(bf16), vt) / l
                lse = m + jnp.log(l)
                if gated:
                    o = o * jax.nn.sigmoid(lse - sink_ref[h])
                else:
                    lse_tile = jnp.where(lane == h, lse, lse_tile)
                halves.append(o)
            o_ref[rows, cols] = jnp.where(low, halves[0], halves[1]).astype(bf16)
        if not gated:
            lse_ref[rows, :] = lse_tile


def _band_attention(q, k, v, bias, sinks=None):
    rows, width = q.shape
    d = width // A_WIDTH
    per_tile = Q_TILE // BLOCK
    cur = pl.BlockSpec((Q_TILE, A_WIDTH), lambda c, j: (j, c))
    prev = pl.BlockSpec((BLOCK, A_WIDTH), lambda c, j: (jnp.maximum(j * per_tile - 1, 0), c))
    bias_spec = pl.BlockSpec(bias.shape, lambda c, j: (0, 0, 0, 0))
    gated = sinks is not None
    in_specs = [cur, prev, cur, prev, cur, bias_spec]
    args = [q, k, k, v, v, bias]
    out_specs = [cur]
    out_shape = [jax.ShapeDtypeStruct((rows, width), bf16)]
    if gated:
        in_specs = [pl.BlockSpec(memory_space=pltpu.SMEM)] + in_specs
        args = [sinks] + args
    else:
        out_specs.append(pl.BlockSpec((Q_TILE, LANES), lambda c, j: (j, c)))
        out_shape.append(jax.ShapeDtypeStruct((rows, d * LANES), f32))
    outs = pl.pallas_call(
        functools.partial(_band_attn_kernel, gated),
        grid=(d, rows // Q_TILE),
        in_specs=in_specs, out_specs=out_specs, out_shape=out_shape,
        compiler_params=pltpu.CompilerParams(dimension_semantics=("arbitrary", "arbitrary"),
                                             vmem_limit_bytes=VMEM_LIMIT),
        name=f"band_attn_d{d}" + ("_gated" if gated else ""),
    )(*args)
    return outs[0] if gated else tuple(outs)


def _pad_rows(new, at_end):
    z = jnp.zeros((LANES - new.shape[0], new.shape[1]), f32)
    return jnp.concatenate([z, new] if at_end else [new, z], axis=0)


def _softmax_rows(s):
    m = jnp.max(s, axis=-1, keepdims=True)
    p = jnp.exp(s - m)
    l = jnp.sum(p, axis=-1, keepdims=True)
    return p, m, l


def _cache_scores(qbd, kt_ref, k_new, tab_ref):
    s = jnp.concatenate([_dot(qbd, kt_ref[...].astype(bf16)),
                         _dot_nt(qbd, _pad_rows(k_new, False).astype(bf16))], axis=1) + tab_ref[...]
    p, m, l = _softmax_rows(s)
    return p.astype(bf16), m, l


def _cache_values(pb, l, vt_ref, v_new):
    buf = vt_ref.shape[1]
    res = _dot_nt(pb[:, :buf], vt_ref[...].astype(bf16)) + _dot(pb[:, buf:], _pad_rows(v_new, False).astype(bf16))
    return res / l


def _shift_into(dst_ref, ct_ref, new):
    w, buf = ct_ref.shape
    t = new.shape[0]
    new_t = jnp.transpose(_pad_rows(new, True))
    rolled = pltpu.roll(ct_ref[...], buf - t, axis=1)
    if buf > LANES:
        dst_ref[:, 0:buf - LANES] = rolled[:, 0:buf - LANES]
    lane = lax.broadcasted_iota(jnp.int32, (w, LANES), 1)
    dst_ref[:, buf - LANES:] = jnp.where(lane >= LANES - t, new_t, rolled[:, buf - LANES:])


def _own_head_mask(t):
    rows = HEADS * t
    row_head = lax.broadcasted_iota(jnp.int32, (rows, A_WIDTH), 0) >> (t.bit_length() - 1)
    col_head = lax.broadcasted_iota(jnp.int32, (rows, A_WIDTH), 1) >> (HEAD_DIM.bit_length() - 1)
    return row_head == col_head


def _out_proj_stage(x, mix_b, wo_ref, gpost_ref, gpre2_ref):
    x1 = x + _rms(_dot(mix_b, wo_ref[...]), gpost_ref[...])
    return x1, _rms(x1, gpre2_ref[...]).astype(bf16)


def _ffn_up(hb, wup_ref, c):
    a = jnp.maximum(_dot(hb, wup_ref[:, c * FF_CHUNK:(c + 1) * FF_CHUNK]), 0.0)
    return (a * a).astype(bf16)


def _ffn_down(acc, a2, wdn_ref, c):
    return acc + _dot(a2, wdn_ref[c * FF_CHUNK:(c + 1) * FF_CHUNK, :])


def _ungroup(blk_ref, w, nat_s, g4_s):
    rows, width = blk_ref.shape
    slabs, d = w // LANES, width // w
    if d == 1:
        return blk_ref[...].astype(f32)
    if d == GROUP * GROUP:
        for c in range(d):
            r, r2 = c % GROUP, c // GROUP
            for s in range(slabs):
                col = (c * slabs + s) * LANES
                g4_s[r * slabs + s, pl.ds(r2, rows, stride=GROUP), :] = blk_ref[:, col:col + LANES].astype(f32)
        for r in range(GROUP):
            for s in range(slabs):
                nat_s[s, pl.ds(r, rows * GROUP, stride=GROUP), :] = g4_s[r * slabs + s]
    else:
        assert d == GROUP
        for r in range(GROUP):
            for s in range(slabs):
                col = (r * slabs + s) * LANES
                nat_s[s, pl.ds(r, rows, stride=GROUP), :] = blk_ref[:, col:col + LANES].astype(f32)
    return jnp.concatenate([nat_s[s] for s in range(slabs)], axis=1)


def _prompt_finish_kernel(layer, n_alias, x_ref, o1_ref, o4_ref, o16_ref, l1_ref, l4_ref, l16_ref, ob_ref, oc_ref,
                          qa_ref, ka_ref, va_ref, exp_ref, taba_ref, wo_ref, gpost_ref, gpre2_ref, wup_ref, wdn_ref,
                          gpost2_ref, cak_hbm, cav_hbm, *rest):
    out_ref, oa_ref, nak_hbm, nav_hbm, nat_s, g4_s, in_buf, out_buf, in_sem, out_sem = rest[n_alias:]
    step, n_steps = pl.program_id(0), pl.num_programs(0)
    nseq, t = qa_ref.shape[0], qa_ref.shape[1]
    n_units = 2 * nseq
    own = _own_head_mask(t)
    st = {}
    base = step * nseq
    first_slot = lax.rem(step * n_units, READ_SLOTS)

    def read_slot(u):
        return lax.rem(first_slot + u, READ_SLOTS)

    def read(u):
        slot = read_slot(u)
        src = (cak_hbm, cav_hbm)[u % 2].at[layer, base + u // 2]
        return pltpu.make_async_copy(src, in_buf.at[slot], in_sem.at[slot])

    def write(u):
        dst = (nak_hbm, nav_hbm)[u % 2].at[layer, base + u // 2]
        return pltpu.make_async_copy(out_buf.at[u % 2], dst, out_sem.at[u % 2])

    def merge_stage():
        lses = [_ungroup(l_ref, LANES, nat_s, g4_s) for l_ref in (l1_ref, l4_ref, l16_ref)]
        m = jnp.maximum(jnp.maximum(lses[0], lses[1]), lses[2])
        es = [jnp.exp(l - m) for l in lses]
        tot = es[0] + es[1] + es[2]
        oa = jnp.zeros((x_ref.shape[0], A_WIDTH), f32)
        for e, o_ref in zip(es, (o1_ref, o4_ref, o16_ref)):
            w = e / tot
            w_hi = w.astype(bf16)
            w_lo = (w - w_hi.astype(f32)).astype(bf16)
            wx = _dot(jnp.concatenate([w_hi, w_lo], axis=1), exp_ref[...])
            oa = oa + wx * _ungroup(o_ref, A_WIDTH, nat_s, g4_s)
        st["mix"] = jnp.concatenate([oa.astype(bf16), ob_ref[...], oc_ref[...]], axis=1)

    def proj_stage():
        st["x1"], st["hb"] = _out_proj_stage(x_ref[...], st["mix"], wo_ref, gpost_ref, gpre2_ref)
        st["acc"] = jnp.zeros(x_ref.shape, f32)

    def up_stage(c):
        st["a", c] = _ffn_up(st["hb"], wup_ref, c)

    def down_stage(c):
        st["acc"] = _ffn_down(st["acc"], st.pop(("a", c)), wdn_ref, c)

    def final_stage():
        out_ref[...] = st["x1"] + _rms(st["acc"], gpost2_ref[...])

    n_chunks = wup_ref.shape[1] // FF_CHUNK
    pairs = [[functools.partial(up_stage, c), functools.partial(down_stage, c - 1)] for c in range(1, n_chunks)]
    share, extra = divmod(len(pairs), n_units - 1)
    stages = [[merge_stage, proj_stage, functools.partial(up_stage, 0)]]
    for u in range(n_units - 1):
        stages.append([f for _ in range(share + (1 if u < extra else 0)) for f in pairs.pop(0)])
    stages[-1] += [functools.partial(down_stage, n_chunks - 1), final_stage]

    def keys_unit(s, src_ref):
        q6 = jnp.concatenate([qa_ref[s]] * HEADS, axis=0)
        qbd = jnp.where(own, q6, 0.0).astype(bf16)
        st["p"], _, st["l"] = _cache_scores(qbd, src_ref, ka_ref[s], taba_ref)
        _shift_into(out_buf.at[0], src_ref, ka_ref[s])

    def values_unit(s, src_ref):
        res = jnp.where(own, _cache_values(st["p"], st["l"], src_ref, va_ref[s]), 0.0)
        oa = res[0:t, :]
        for h in range(1, HEADS):
            oa = oa + res[h * t:(h + 1) * t, :]
        oa_ref[s] = oa
        _shift_into(out_buf.at[1], src_ref, va_ref[s])

    @pl.when(step == 0)
    def _():
        for u in range(READ_AHEAD):
            read(u).start()

    for u in range(n_units):
        if u + READ_AHEAD < n_units:
            read(u + READ_AHEAD).start()
        else:
            @pl.when(step + 1 < n_steps)
            def _():
                read(u + READ_AHEAD).start()
        read(u).wait()
        if u >= 2:
            write(u - 2).wait()
        else:
            @pl.when(step > 0)
            def _():
                write(u).wait()
        for stage in stages[u]:
            stage()
        (keys_unit if u % 2 == 0 else values_unit)(u // 2, in_buf.at[read_slot(u)])
        write(u).start()

    @pl.when(step == n_steps - 1)
    def _():
        write(n_units - 2).wait()
        write(n_units - 1).wait()


def _prompt_finish(layer, x, row_inputs, sample_qkv, expand, taba, params, caches, prev_outs):
    s, d = x.shape
    t = ROW_TILE
    n_steps = s // t
    n_seq = sample_qkv[0].shape[0]
    assert n_seq % n_steps == 0
    per_step = n_seq // n_steps
    row = lambda a: pl.BlockSpec((a.shape[0] * t // s, a.shape[1]), lambda i: (i, 0))
    seqs = lambda a: pl.BlockSpec((per_step,) + a.shape[1:], lambda i: (i, 0, 0))
    const = lambda a: pl.BlockSpec(a.shape, lambda i: (0, 0), pipeline_mode=pl.Buffered(1))
    gain = lambda a: pl.BlockSpec((None, 1, a.shape[2]), lambda i: (layer, 0, 0))
    w_out, g_post, g_pre2, w_up, w_down, g_post2 = params
    cak, cav = caches
    n_alias = 0 if prev_outs is None else len(prev_outs)
    any_spec = pl.BlockSpec(memory_space=pl.ANY)
    args = ([x] + row_inputs + list(sample_qkv)
            + [expand, taba, w_out, g_post, g_pre2, w_up, w_down, g_post2, cak, cav]
            + (list(prev_outs) if n_alias else []))
    in_specs = ([row(a) for a in [x] + row_inputs] + [seqs(a) for a in sample_qkv]
                + [const(expand), const(taba), _layer_block(w_out, layer), gain(g_post), gain(g_pre2),
                   _layer_block(w_up, layer), _layer_block(w_down, layer), gain(g_post2), any_spec, any_spec]
                + [any_spec] * n_alias)
    first_alias = len(args) - n_alias
    assert (2 * per_step) % WRITE_SLOTS == 0
    outs = pl.pallas_call(
        functools.partial(_prompt_finish_kernel, layer, n_alias),
        grid=(n_steps,),
        in_specs=in_specs,
        out_specs=[pl.BlockSpec((t, d), lambda i: (i, 0)), seqs(sample_qkv[0]), any_spec, any_spec],
        out_shape=[jax.ShapeDtypeStruct((s, d), f32), jax.ShapeDtypeStruct(sample_qkv[0].shape, f32),
                   jax.ShapeDtypeStruct(cak.shape, f32), jax.ShapeDtypeStruct(cav.shape, f32)],
        scratch_shapes=[pltpu.VMEM((A_WIDTH // LANES, t, LANES), f32),
                        pltpu.VMEM((GROUP * A_WIDTH // LANES, t // GROUP, LANES), f32),
                        pltpu.VMEM((READ_SLOTS,) + cak.shape[2:], f32), pltpu.VMEM((WRITE_SLOTS,) + cak.shape[2:], f32),
                        pltpu.SemaphoreType.DMA((READ_SLOTS,)), pltpu.SemaphoreType.DMA((WRITE_SLOTS,))],
        input_output_aliases={first_alias + i: 2 + i for i in range(n_alias)},
        compiler_params=pltpu.CompilerParams(dimension_semantics=("arbitrary",),
                                             vmem_limit_bytes=VMEM_LIMIT_STREAMING),
        name=f"prompt_finish_l{layer}",
    )(*args)
    return outs[0], outs[1], (outs[2], outs[3])


def _sample_finish_kernel(x_ref, mix_ref, wo_ref, gpost_ref, gpre2_ref, wup_ref, wdn_ref, gpost2_ref, out_ref):
    x1, hb = _out_proj_stage(x_ref[...], mix_ref[...].astype(bf16), wo_ref, gpost_ref, gpre2_ref)
    acc = jnp.zeros(x1.shape, f32)
    n_chunks = wup_ref.shape[1] // FF_CHUNK
    a2 = _ffn_up(hb, wup_ref, 0)
    for c in range(n_chunks):
        a2_next = _ffn_up(hb, wup_ref, c + 1) if c + 1 < n_chunks else None
        acc = _ffn_down(acc, a2, wdn_ref, c)
        a2 = a2_next
    out_ref[...] = x1 + _rms(acc, gpost2_ref[...])


def _sample_finish(layer, x, mix, params):
    s, d = x.shape
    t = ROW_TILE
    row = lambda a: pl.BlockSpec((t, a.shape[1]), lambda i: (i, 0))
    gain = lambda a: pl.BlockSpec((None, 1, a.shape[2]), lambda i: (layer, 0, 0))
    w_out, g_post, g_pre2, w_up, w_down, g_post2 = params
    return pl.pallas_call(
        _sample_finish_kernel,
        grid=(s // t,),
        in_specs=[row(x), row(mix), _layer_block(w_out, layer), gain(g_post), gain(g_pre2),
                  _layer_block(w_up, layer), _layer_block(w_down, layer), gain(g_post2)],
        out_specs=pl.BlockSpec((t, d), lambda i: (i, 0)),
        out_shape=jax.ShapeDtypeStruct((s, d), f32),
        compiler_params=pltpu.CompilerParams(dimension_semantics=("arbitrary",),
                                             vmem_limit_bytes=VMEM_LIMIT),
        name="sample_finish",
    )(x, mix, w_out, g_post, g_pre2, w_up, w_down, g_post2)


def _sample_proj_kernel(x_ref, g_ref, w_ref, qa_ref, ka_ref, va_ref, qb_ref, kb_ref, vb_ref, cb_ref, u_ref):
    aq, ak, av, bq, bk, bv, cb, cc, cx = _project(x_ref, g_ref, w_ref)
    qa_ref[...] = aq * SCALE
    ka_ref[...] = ak
    va_ref[...] = av
    qb_ref[...] = bq * SCALE
    kb_ref[...] = bk
    vb_ref[...] = bv
    cb_ref[...] = cb
    u_ref[...] = cc * cx


def _sample_proj(layer, x, g, w):
    s, d = x.shape
    t = ROW_TILE
    widths = (A_WIDTH, A_WIDTH, A_WIDTH, A_WIDTH, B_KV_WIDTH, B_KV_WIDTH, C_WIDTH, C_WIDTH)
    row = lambda width: pl.BlockSpec((t, width), lambda i: (i, 0))
    const = lambda shape: pl.BlockSpec(shape, lambda i: (0, 0))
    return pl.pallas_call(
        _sample_proj_kernel,
        grid=(s // t,),
        in_specs=[row(d), const((1, d)), _layer_block(w, layer)],
        out_specs=[row(wd) for wd in widths],
        out_shape=[jax.ShapeDtypeStruct((s, wd), f32) for wd in widths],
        compiler_params=pltpu.CompilerParams(dimension_semantics=("arbitrary",),
                                             vmem_limit_bytes=VMEM_LIMIT),
        name="sample_proj",
    )(x, g, w)


def _sample_rest_kernel(n_alias, sink_ref, oa_ref, qb_ref, kb_ref, vb_ref, cb_ref, u_ref,
                        cbk_ref, cbv_ref, cst_ref, tabb_ref, cw_ref, *rest):
    mix_ref, nbk_ref, nbv_ref, nst_ref, uext_s = rest[n_alias:]
    t = qb_ref.shape[1]
    rows = HEADS * t
    low = _low_half((t, LANES))
    zero = jnp.zeros((t, LANES), f32)
    row_head = lax.broadcasted_iota(jnp.int32, (rows, 1), 0) >> (t.bit_length() - 1)
    sink_col = jnp.zeros((rows, 1), f32)
    for h in range(HEADS):
        sink_col = jnp.where(row_head == h, sink_ref[h], sink_col)
    cw = cw_ref[...]
    n_seq = qb_ref.shape[0]
    scores = []
    for g in range(n_seq):
        qb = qb_ref[g]
        t0, t1, t2 = (qb[:, i * LANES:(i + 1) * LANES] for i in range(3))
        qbd = jnp.concatenate([
            jnp.where(low, t0, zero), jnp.where(low, _swap_halves(t0), zero), jnp.where(low, t1, zero),
            jnp.where(low, zero, t1), jnp.where(low, zero, _swap_halves(t2)), jnp.where(low, zero, t2)],
            axis=0).astype(bf16)
        scores.append(_cache_scores(qbd, cbk_ref.at[g], kb_ref[g], tabb_ref))
    outs = [_cache_values(pb, lb, cbv_ref.at[g], vb_ref[g]) for g, (pb, _, lb) in enumerate(scores)]
    for g in range(n_seq):
        _, mb, lb = scores[g]
        resb = outs[g] * jax.nn.sigmoid(mb + jnp.log(lb) - sink_col)
        r = [resb[h * t:(h + 1) * t, :] for h in range(HEADS)]
        ob = jnp.concatenate([jnp.where(low, r[0], _swap_halves(r[1])), jnp.where(low, r[2], r[3]),
                              jnp.where(low, _swap_halves(r[4]), r[5])], axis=1)
        _shift_into(nbk_ref.at[g], cbk_ref.at[g], kb_ref[g])
        _shift_into(nbv_ref.at[g], cbv_ref.at[g], vb_ref[g])

        u = u_ref[g]
        uext_s[g, SUBLANES - (CONV_WIDTH - 1):SUBLANES, :] = cst_ref[g]
        uext_s[g, SUBLANES:, :] = u
        u1 = uext_s[g, pl.ds(SUBLANES - 1, t), :]
        u2 = uext_s[g, pl.ds(SUBLANES - 2, t), :]
        oc = cb_ref[g] * (cw[0:1, :] * u2 + cw[1:2, :] * u1 + cw[2:3, :] * u)
        nst_ref[g] = uext_s[g, pl.ds(SUBLANES + t - (CONV_WIDTH - 1), CONV_WIDTH - 1), :]
        mix_ref[g] = jnp.concatenate([oa_ref[g], ob, oc], axis=1)


def _sample_rest(layer, sinks, oa, proj, caches, tabb, cw, prev_outs):
    cbk, cbv, cst = caches
    n, t = oa.shape[0], oa.shape[1]
    g = SEQS_PER_STEP
    proj3 = [a.reshape(n, t, a.shape[1]) for a in proj]
    per_seq = lambda a: pl.BlockSpec((g,) + a.shape[1:], lambda i: (i, 0, 0))
    per_layer_seq = lambda a: pl.BlockSpec((None, g) + a.shape[2:], lambda i: (layer, i, 0, 0))
    const = lambda a: pl.BlockSpec(a.shape, lambda i: (0, 0))
    n_alias = 0 if prev_outs is None else len(prev_outs)
    args = [sinks, oa] + proj3 + [cbk, cbv, cst, tabb, cw] + (list(prev_outs) if n_alias else [])
    in_specs = ([pl.BlockSpec(memory_space=pltpu.SMEM), per_seq(oa)] + [per_seq(a) for a in proj3]
                + [per_layer_seq(a) for a in caches] + [const(tabb), const(cw)]
                + [pl.BlockSpec(memory_space=pl.ANY)] * n_alias)
    first_alias = len(args) - n_alias
    outs = pl.pallas_call(
        functools.partial(_sample_rest_kernel, n_alias),
        grid=(n // g,),
        in_specs=in_specs,
        out_specs=[pl.BlockSpec((g, t, 4 * C_WIDTH), lambda i: (i, 0, 0)), per_layer_seq(cbk), per_layer_seq(cbv),
                   pl.BlockSpec((g, CONV_WIDTH - 1, C_WIDTH), lambda i: (i, 0, 0))],
        out_shape=[jax.ShapeDtypeStruct((n, t, 4 * C_WIDTH), f32), jax.ShapeDtypeStruct(cbk.shape, f32),
                   jax.ShapeDtypeStruct(cbv.shape, f32), jax.ShapeDtypeStruct((n, CONV_WIDTH - 1, C_WIDTH), f32)],
        scratch_shapes=[pltpu.VMEM((g, SUBLANES + t, C_WIDTH), f32)],
        input_output_aliases={first_alias + i: 1 + i for i in range(n_alias)},
        compiler_params=pltpu.CompilerParams(dimension_semantics=("arbitrary",),
                                             vmem_limit_bytes=VMEM_LIMIT),
        name=f"sample_rest_l{layer}",
    )(*args)
    mix, nbk, nbv, nst = outs
    return mix.reshape(n * t, 4 * C_WIDTH), (nbk, nbv), nst


def _t5_bucket(dist):
    dist = jnp.maximum(dist, 0)
    max_exact = NUM_BUCKETS // 2
    scaled = jnp.log(jnp.maximum(dist, 1).astype(f32) / max_exact) / math.log(MAX_DISTANCE / max_exact)
    large = max_exact + (scaled * (NUM_BUCKETS - max_exact)).astype(jnp.int32)
    large = jnp.minimum(large, NUM_BUCKETS - 1)
    return jnp.where(dist < max_exact, dist, large)


def _bias_of(table, dist):
    onehot = jax.nn.one_hot(_t5_bucket(dist), NUM_BUCKETS, dtype=f32)
    return jnp.einsum('...b,bh->h...', onehot, table.astype(f32), precision=lax.Precision.HIGHEST)


def _band_bias(table, d):
    qi = jnp.arange(BLOCK)[:, None]
    kj = jnp.arange(2 * BLOCK)[None, :]
    delta = BLOCK + qi - kj
    in_band = (delta >= 0) & (delta <= BLOCK)
    bias = _bias_of(table, delta * d)
    with_prev = jnp.where(in_band[None], bias, NEG)
    no_prev = jnp.where((in_band & (kj >= BLOCK))[None], bias, NEG)
    return jnp.stack([no_prev, with_prev], axis=0)


def _sample_table_a(table, t, buf, cols):
    i = jnp.arange(t)[:, None]
    r = jnp.arange(cols)[None, :]
    delta = buf + i - r
    count = sum(((delta >= 0) & (delta % d == 0) & (delta // d <= w // d)).astype(f32) for w, d in A_PATTERNS)
    tab = jnp.where((count > 0)[None], _bias_of(table, delta) + jnp.log(jnp.maximum(count, 1.0))[None], NEG)
    return tab.reshape(table.shape[1] * t, cols)


def _sample_table_b(table, t, buf, cols):
    i = jnp.arange(t)[:, None]
    r = jnp.arange(cols)[None, :]
    delta = buf + i - r
    valid = (delta >= 0) & (delta <= B_WINDOW)
    return jnp.where(valid[None], _bias_of(table, delta), NEG).reshape(table.shape[1] * t, cols)


def _expand_matrix():
    lane = jnp.arange(LANES)[:, None]
    col_head = jnp.arange(A_WIDTH)[None, :] // HEAD_DIM
    e = (lane == col_head).astype(bf16)
    return jnp.concatenate([e, e], axis=0)


def kernel(x_prompt, x_sample, cache_a_k, cache_a_v, cache_b_k, cache_b_v, state_conv, w_in, w_out, conv_w,
           b_sinks, rel_bias, g_mix_pre, g_mix_post, w_up, w_down, g_mlp_pre, g_mlp_post):
    depth = w_in.shape[0]
    batch, seq, d_model = x_prompt.shape
    n_dec, t_dec, _ = x_sample.shape
    buf_a, buf_b = cache_a_k.shape[2], cache_b_k.shape[2]
    assert batch == 1 and seq % (A_PATTERNS[-1][1] * BLOCK) == 0 and seq >= A_WINDOW
    assert buf_a == A_WINDOW and buf_b == B_WINDOW and t_dec == SUBLANES
    assert (n_dec * t_dec) % ROW_TILE == 0 and seq % ROW_TILE == 0 and n_dec % SEQS_PER_STEP == 0
    assert tuple(d for _, d in A_PATTERNS) == (1, GROUP, GROUP * GROUP) and seq % (GROUP * GROUP * Q_TILE) == 0

    xp = x_prompt.reshape(seq, d_model)
    xs = x_sample.reshape(n_dec * t_dec, d_model)
    to_fm = lambda c: jnp.transpose(c, (0, 1, 3, 4, 2)).reshape(depth, n_dec, c.shape[3] * HEAD_DIM, c.shape[2])
    from_fm = lambda c, heads: jnp.transpose(c.reshape(c.shape[:2] + (heads, HEAD_DIM, c.shape[3])), (0, 1, 4, 2, 3))
    caches_a = (to_fm(cache_a_k), to_fm(cache_a_v))
    caches_b = (to_fm(cache_b_k), to_fm(cache_b_v), state_conv)
    table_a, table_b = rel_bias[:, :HEADS], rel_bias[:, HEADS:]
    band_a = [_band_bias(table_a, d) for _, d in A_PATTERNS]
    band_b = _band_bias(table_b, 1)
    tab_sa = _sample_table_a(table_a, t_dec, buf_a, buf_a + LANES)
    tab_sb = _sample_table_b(table_b, t_dec, buf_b, buf_b + LANES)
    expand = _expand_matrix()
    row = lambda v: v.reshape(1, -1).astype(f32)
    gains = lambda v: v.reshape(depth, 1, -1).astype(f32)
    w_in_b = w_in.astype(bf16)
    params = (w_out.astype(bf16), gains(g_mix_post), gains(g_mlp_pre), w_up.astype(bf16), w_down.astype(bf16),
              gains(g_mlp_post))

    new_p = [[] for _ in range(5)]
    conv_s = []
    new_a = new_b = None
    for l in range(depth):
        cw = conv_w[l].astype(f32)
        sinks = b_sinks[l].astype(f32)

        (qa1, qa4, qa16, ka1, ka4, ka16, va1, va4, va16, qb, kbx, vbx, oc,
         kat, vat, kbt, vbt, ut) = _prompt_proj(l, xp, row(g_mix_pre[l]), w_in_b, cw)
        pats = [_band_attention(q, k, v, bias) for q, k, v, bias in
                ((qa1, ka1, va1, band_a[0]), (qa4, ka4, va4, band_a[1]), (qa16, ka16, va16, band_a[2]))]
        ob = _band_attention(qb, kbx, vbx, band_b, sinks=sinks)
        for lst, a in zip(new_p, (kat[None, None], vat[None, None], kbt[None, None], vbt[None, None],
                                  ut.reshape(1, CONV_WIDTH - 1, C_WIDTH))):
            lst.append(a)

        proj_s = _sample_proj(l, xs, row(g_mix_pre[l]), w_in_b)
        qkv_s = [a.reshape(n_dec, t_dec, A_WIDTH) for a in proj_s[:3]]
        xp, oa_s, new_a = _prompt_finish(l, xp, [p[0] for p in pats] + [p[1] for p in pats] + [ob, oc], qkv_s,
                                         expand, tab_sa, params, caches_a, new_a)

        mix_s, new_b, nst = _sample_rest(l, sinks, oa_s, proj_s[3:], caches_b, tab_sb, cw, new_b)
        conv_s.append(nst)
        xs = _sample_finish(l, xs, mix_s, params)

    a_k_p, a_v_p, b_k_p, b_v_p = [jnp.concatenate(t, axis=0) for t in new_p[:4]]
    return (xp.reshape(batch, seq, d_model), xs.reshape(n_dec, t_dec, d_model),
            from_fm(a_k_p, HEADS), from_fm(a_v_p, HEADS), from_fm(b_k_p, 2), from_fm(b_v_p, 2),
            jnp.stack(new_p[4], axis=0),
            from_fm(new_a[0], HEADS), from_fm(new_a[1], HEADS), from_fm(new_b[0], 2), from_fm(new_b[1], 2),
            jnp.stack(conv_s, axis=0))
```

```python
import functools
import math

import jax
import jax.numpy as jnp
from jax import lax
from jax.experimental import pallas as pl
from jax.experimental.pallas import tpu as pltpu

HEAD_DIM = 64
HEADS = 6
A_WIDTH = HEADS * HEAD_DIM
B_KV_WIDTH = 2 * HEAD_DIM
C_WIDTH = 256
A_PATTERNS = ((128, 1), (512, 4), (2048, 16))
A_WINDOW = 2048
B_WINDOW = 128
BLOCK = 128
CONV_WIDTH = 3
NUM_BUCKETS = 32
MAX_DISTANCE = 2048
EPS = 1e-6
SCALE = 1.0 / math.sqrt(HEAD_DIM)
NEG = -1e30
LANES = 128
SUBLANES = 8
ROW_TILE = 512
PROJ_TILE = 1024
Q_TILE = 1024
GROUP = 4
FF_CHUNK = 512
SEQS_PER_STEP = 16
READ_AHEAD = 2
READ_SLOTS = READ_AHEAD + 1
WRITE_SLOTS = 2
VMEM_LIMIT = 56 * 1024 * 1024
VMEM_LIMIT_STREAMING = 63 * 1024 * 1024

_IN_SIZES = (A_WIDTH, A_WIDTH, A_WIDTH, A_WIDTH, B_KV_WIDTH, B_KV_WIDTH, C_WIDTH, C_WIDTH, C_WIDTH)
_IN_OFFS = tuple(sum(_IN_SIZES[:i]) for i in range(len(_IN_SIZES) + 1))

f32 = jnp.float32
bf16 = jnp.bfloat16


def _rms(x, g):
    ms = jnp.mean(x * x, axis=-1, keepdims=True)
    return (x * lax.rsqrt(ms + EPS)) * g


def _dot(a, b):
    return jnp.dot(a, b, preferred_element_type=f32)


def _dot_nt(a, b):
    return lax.dot_general(a, b, (((1,), (1,)), ((), ())), preferred_element_type=f32)


def _project(x_ref, g_ref, w_ref):
    hb = _rms(x_ref[...], g_ref[...]).astype(bf16)
    proj = _dot(hb, w_ref[...])
    return [proj[:, _IN_OFFS[i]:_IN_OFFS[i + 1]] for i in range(len(_IN_SIZES))]


def _swap_halves(t):
    return pltpu.roll(t, HEAD_DIM, axis=1)


def _low_half(shape):
    return lax.broadcasted_iota(jnp.int32, shape, 1) < HEAD_DIM


def _layer_block(a, layer):
    return pl.BlockSpec((None,) + a.shape[1:], lambda *_: (layer, 0, 0), pipeline_mode=pl.Buffered(1))


def _write_grouped(x, nat_ref, g4_ref, g16_ref, nat_s, g4_s):
    t, w = x.shape
    slabs = w // LANES
    nat_ref[...] = x.astype(bf16)
    for s in range(slabs):
        nat_s[s] = x[:, s * LANES:(s + 1) * LANES]
    for r in range(GROUP):
        for s in range(slabs):
            g = nat_s[s, pl.ds(r, t // GROUP, stride=GROUP), :]
            g4_s[r * slabs + s] = g
            g4_ref[:, (r * slabs + s) * LANES:(r * slabs + s + 1) * LANES] = g.astype(bf16)
    for c in range(GROUP * GROUP):
        r, r2 = c % GROUP, c // GROUP
        for s in range(slabs):
            h = g4_s[r * slabs + s, pl.ds(r2, t // (GROUP * GROUP), stride=GROUP), :]
            g16_ref[:, (c * slabs + s) * LANES:(c * slabs + s + 1) * LANES] = h.astype(bf16)


def _prompt_proj_kernel(n_steps, x_ref, g_ref, w_ref, cw_ref,
                        qa1_ref, qa4_ref, qa16_ref, ka1_ref, ka4_ref, ka16_ref, va1_ref, va4_ref, va16_ref,
                        qb_ref, kbx_ref, vbx_ref, oc_ref,
                        kat_ref, vat_ref, kbt_ref, vbt_ref, ut_ref, uext_ref, nat_s, g4_s):
    t = x_ref.shape[0]
    step = pl.program_id(0)
    aq, ak, av, bq, bk, bv, cb, cc, cx = _project(x_ref, g_ref, w_ref)
    _write_grouped(aq * SCALE, qa1_ref, qa4_ref, qa16_ref, nat_s, g4_s)
    _write_grouped(ak, ka1_ref, ka4_ref, ka16_ref, nat_s, g4_s)
    _write_grouped(av, va1_ref, va4_ref, va16_ref, nat_s, g4_s)
    qb_ref[...] = (bq * SCALE).astype(bf16)
    low = _low_half(bk.shape)

    def widen(kv):
        sw = _swap_halves(kv)
        return jnp.concatenate([jnp.where(low, kv, sw), kv, jnp.where(low, sw, kv)], axis=1)

    kbx_ref[...] = widen(bk).astype(bf16)
    vbx_ref[...] = widen(bv).astype(bf16)

    @pl.when(step >= n_steps - A_WINDOW // t)
    def _():
        kat_ref[...] = ak.T
        vat_ref[...] = av.T

    @pl.when(step == 0)
    def _():
        uext_ref[0:SUBLANES, :] = jnp.zeros((SUBLANES, C_WIDTH), f32)

    u = cc * cx
    uext_ref[SUBLANES:, :] = u
    u1 = uext_ref[pl.ds(SUBLANES - 1, t), :]
    u2 = uext_ref[pl.ds(SUBLANES - 2, t), :]
    cw = cw_ref[...]
    conv = cw[0:1, :] * u2 + cw[1:2, :] * u1 + cw[2:3, :] * u
    oc_ref[...] = (cb * conv).astype(bf16)

    @pl.when(step == n_steps - 1)
    def _():
        kbt_ref[...] = bk[t - B_WINDOW:, :].T
        vbt_ref[...] = bv[t - B_WINDOW:, :].T
        ut_ref[...] = uext_ref[pl.ds(t + SUBLANES - (CONV_WIDTH - 1), CONV_WIDTH - 1), :]

    uext_ref[0:SUBLANES, :] = u[t - SUBLANES:, :]


def _prompt_proj(layer, x, g, w, cw):
    s, d = x.shape
    t = PROJ_TILE
    n = s // t
    tail_blocks = A_WINDOW // t
    row = lambda width: pl.BlockSpec((t, width), lambda i: (i, 0))
    const = lambda shape: pl.BlockSpec(shape, lambda i: (0, 0))
    tail = pl.BlockSpec((A_WIDTH, t), lambda i: (0, jnp.maximum(i - (n - tail_blocks), 0)))
    grouped_specs, grouped_shapes = [], []
    for _ in range(3):
        for _, dil in A_PATTERNS:
            grouped_specs.append(pl.BlockSpec((t // dil, dil * A_WIDTH), lambda i: (i, 0)))
            grouped_shapes.append(jax.ShapeDtypeStruct((s // dil, dil * A_WIDTH), bf16))
    bfo = lambda width: jax.ShapeDtypeStruct((s, width), bf16)
    return pl.pallas_call(
        functools.partial(_prompt_proj_kernel, n),
        grid=(n,),
        in_specs=[row(d), const((1, d)), _layer_block(w, layer), const(cw.shape)],
        out_specs=grouped_specs + [row(A_WIDTH)] * 3 + [row(C_WIDTH), tail, tail,
                                                        const((B_KV_WIDTH, B_WINDOW)), const((B_KV_WIDTH, B_WINDOW)),
                                                        const((CONV_WIDTH - 1, C_WIDTH))],
        out_shape=grouped_shapes + [bfo(A_WIDTH)] * 3 + [bfo(C_WIDTH),
                                                         jax.ShapeDtypeStruct((A_WIDTH, A_WINDOW), f32),
                                                         jax.ShapeDtypeStruct((A_WIDTH, A_WINDOW), f32),
                                                         jax.ShapeDtypeStruct((B_KV_WIDTH, B_WINDOW), f32),
                                                         jax.ShapeDtypeStruct((B_KV_WIDTH, B_WINDOW), f32),
                                                         jax.ShapeDtypeStruct((CONV_WIDTH - 1, C_WIDTH), f32)],
        scratch_shapes=[pltpu.VMEM((t + SUBLANES, C_WIDTH), f32),
                        pltpu.VMEM((A_WIDTH // LANES, t, LANES), f32),
                        pltpu.VMEM((GROUP * A_WIDTH // LANES, t // GROUP, LANES), f32)],
        compiler_params=pltpu.CompilerParams(dimension_semantics=("arbitrary",),
                                             vmem_limit_bytes=VMEM_LIMIT),
        name="prompt_proj",
    )(x, g, w, cw)


def _band_attn_kernel(gated, *refs):
    if gated:
        sink_ref, q_ref, kp_ref, kc_ref, vp_ref, vc_ref, bias_ref, o_ref = refs
    else:
        q_ref, kp_ref, kc_ref, vp_ref, vc_ref, bias_ref, o_ref, lse_ref = refs
    has_prev = jnp.minimum(pl.program_id(1), 1)
    kcat = jnp.concatenate([kp_ref[...], kc_ref[...]], axis=0)
    vcat = jnp.concatenate([vp_ref[...], vc_ref[...]], axis=0)
    low = _low_half((BLOCK, LANES))
    zero = jnp.zeros((BLOCK, LANES), bf16)
    for b in range(q_ref.shape[0] // BLOCK):
        rows = slice(b * BLOCK, (b + 1) * BLOCK)
        variant = has_prev if b == 0 else 1
        for p in range(HEADS // 2):
            cols = slice(p * LANES, (p + 1) * LANES)
            qt = q_ref[rows, cols]
            kt = kcat[b * BLOCK:(b + 2) * BLOCK, cols]
            vt = vcat[b * BLOCK:(b + 2) * BLOCK, cols]
            halves, lses = [], []
            for e in range(2):
                h = 2 * p + e
                qm = jnp.where(low if e == 0 else ~low, qt, zero)
                s = _dot_nt(qm, kt) + bias_ref[variant, h]
                m = jnp.max(s, axis=-1, keepdims=True)
                pe = jnp.exp(s - m)
                l = jnp.sum(pe, axis=-1, keepdims=True)
                o = _dot(pe.astype(bf16), vt) / l
                lse = m + jnp.log(l)
                if gated:
                    o = o * jax.nn.sigmoid(lse - sink_ref[h])
                halves.append(o)
                lses.append(lse)
            o_ref[rows, cols] = jnp.where(low, halves[0], halves[1]).astype(bf16)
            if not gated:
                lse_ref[rows, cols] = jnp.where(low, lses[0], lses[1])


def _band_attention(q, k, v, bias, sinks=None):
    rows, width = q.shape
    d = width // A_WIDTH
    per_tile = Q_TILE // BLOCK
    cur = pl.BlockSpec((Q_TILE, A_WIDTH), lambda c, j: (j, c))
    prev = pl.BlockSpec((BLOCK, A_WIDTH), lambda c, j: (jnp.maximum(j * per_tile - 1, 0), c))
    bias_spec = pl.BlockSpec(bias.shape, lambda c, j: (0, 0, 0, 0))
    gated = sinks is not None
    in_specs = [cur, prev, cur, prev, cur, bias_spec]
    args = [q, k, k, v, v, bias]
    out_specs = [cur]
    out_shape = [jax.ShapeDtypeStruct((rows, width), bf16)]
    if gated:
        in_specs = [pl.BlockSpec(memory_space=pltpu.SMEM)] + in_specs
        args = [sinks] + args
    else:
        out_specs.append(cur)
        out_shape.append(jax.ShapeDtypeStruct((rows, width), f32))
    outs = pl.pallas_call(
        functools.partial(_band_attn_kernel, gated),
        grid=(d, rows // Q_TILE),
        in_specs=in_specs, out_specs=out_specs, out_shape=out_shape,
        compiler_params=pltpu.CompilerParams(dimension_semantics=("arbitrary", "arbitrary"),
                                             vmem_limit_bytes=VMEM_LIMIT),
        name=f"band_attn_d{d}" + ("_gated" if gated else ""),
    )(*args)
    return outs[0] if gated else tuple(outs)


def _pad_rows(new, at_end):
    z = jnp.zeros((LANES - new.shape[0], new.shape[1]), f32)
    return jnp.concatenate([z, new] if at_end else [new, z], axis=0)


def _softmax_rows(s):
    m = jnp.max(s, axis=-1, keepdims=True)
    p = jnp.exp(s - m)
    l = jnp.sum(p, axis=-1, keepdims=True)
    return p, m, l


def _cache_scores(qbd, kt_ref, k_new, tab_ref):
    s = jnp.concatenate([_dot(qbd, kt_ref[...].astype(bf16)),
                         _dot_nt(qbd, _pad_rows(k_new, False).astype(bf16))], axis=1) + tab_ref[...]
    p, m, l = _softmax_rows(s)
    return p.astype(bf16), m, l


def _cache_values(pb, l, vt_ref, v_new):
    buf = vt_ref.shape[1]
    res = _dot_nt(pb[:, :buf], vt_ref[...].astype(bf16)) + _dot(pb[:, buf:], _pad_rows(v_new, False).astype(bf16))
    return res / l


def _shift_into(dst_ref, ct_ref, new):
    w, buf = ct_ref.shape
    t = new.shape[0]
    new_t = jnp.transpose(_pad_rows(new, True))
    rolled = pltpu.roll(ct_ref[...], buf - t, axis=1)
    if buf > LANES:
        dst_ref[:, 0:buf - LANES] = rolled[:, 0:buf - LANES]
    lane = lax.broadcasted_iota(jnp.int32, (w, LANES), 1)
    dst_ref[:, buf - LANES:] = jnp.where(lane >= LANES - t, new_t, rolled[:, buf - LANES:])


def _own_head_mask(t):
    rows = HEADS * t
    row_head = lax.broadcasted_iota(jnp.int32, (rows, A_WIDTH), 0) >> (t.bit_length() - 1)
    col_head = lax.broadcasted_iota(jnp.int32, (rows, A_WIDTH), 1) >> (HEAD_DIM.bit_length() - 1)
    return row_head == col_head


def _out_proj_stage(x, mix_b, wo_ref, gpost_ref, gpre2_ref):
    x1 = x + _rms(_dot(mix_b, wo_ref[...]), gpost_ref[...])
    return x1, _rms(x1, gpre2_ref[...]).astype(bf16)


def _ffn_up(hb, wup_ref, c):
    a = jnp.maximum(_dot(hb, wup_ref[:, c * FF_CHUNK:(c + 1) * FF_CHUNK]), 0.0)
    return (a * a).astype(bf16)


def _ffn_down(acc, a2, wdn_ref, c):
    return acc + _dot(a2, wdn_ref[c * FF_CHUNK:(c + 1) * FF_CHUNK, :])


def _ungroup(blk_ref, w, nat_s, g4_s):
    rows, width = blk_ref.shape
    slabs, d = w // LANES, width // w
    if d == 1:
        return blk_ref[...].astype(f32)
    if d == GROUP * GROUP:
        for c in range(d):
            r, r2 = c % GROUP, c // GROUP
            for s in range(slabs):
                col = (c * slabs + s) * LANES
                g4_s[r * slabs + s, pl.ds(r2, rows, stride=GROUP), :] = blk_ref[:, col:col + LANES].astype(f32)
        for r in range(GROUP):
            for s in range(slabs):
                nat_s[s, pl.ds(r, rows * GROUP, stride=GROUP), :] = g4_s[r * slabs + s]
    else:
        assert d == GROUP
        for r in range(GROUP):
            for s in range(slabs):
                col = (r * slabs + s) * LANES
                nat_s[s, pl.ds(r, rows, stride=GROUP), :] = blk_ref[:, col:col + LANES].astype(f32)
    return jnp.concatenate([nat_s[s] for s in range(slabs)], axis=1)


def _prompt_finish_kernel(layer, n_alias, x_ref, o1_ref, o4_ref, o16_ref, l1_ref, l4_ref, l16_ref, ob_ref, oc_ref,
                          qa_ref, ka_ref, va_ref, taba_ref, wo_ref, gpost_ref, gpre2_ref, wup_ref, wdn_ref,
                          gpost2_ref, cak_hbm, cav_hbm, *rest):
    out_ref, oa_ref, nak_hbm, nav_hbm, nat_s, g4_s, in_buf, out_buf, in_sem, out_sem = rest[n_alias:]
    step, n_steps = pl.program_id(0), pl.num_programs(0)
    nseq, t = qa_ref.shape[0], qa_ref.shape[1]
    n_units = 2 * nseq
    own = _own_head_mask(t)
    st = {}
    base = step * nseq
    first_slot = lax.rem(step * n_units, READ_SLOTS)

    def read_slot(u):
        return lax.rem(first_slot + u, READ_SLOTS)

    def read(u):
        slot = read_slot(u)
        src = (cak_hbm, cav_hbm)[u % 2].at[layer, base + u // 2]
        return pltpu.make_async_copy(src, in_buf.at[slot], in_sem.at[slot])

    def write(u):
        dst = (nak_hbm, nav_hbm)[u % 2].at[layer, base + u // 2]
        return pltpu.make_async_copy(out_buf.at[u % 2], dst, out_sem.at[u % 2])

    def merge_stage():
        lses = [_ungroup(l_ref, A_WIDTH, nat_s, g4_s) for l_ref in (l1_ref, l4_ref, l16_ref)]
        m = jnp.maximum(jnp.maximum(lses[0], lses[1]), lses[2])
        es = [jnp.exp(l - m) for l in lses]
        tot = es[0] + es[1] + es[2]
        oa = jnp.zeros((x_ref.shape[0], A_WIDTH), f32)
        for e, o_ref in zip(es, (o1_ref, o4_ref, o16_ref)):
            oa = oa + (e / tot) * _ungroup(o_ref, A_WIDTH, nat_s, g4_s)
        st["mix"] = jnp.concatenate([oa.astype(bf16), ob_ref[...], oc_ref[...]], axis=1)

    def proj_stage():
        st["x1"], st["hb"] = _out_proj_stage(x_ref[...], st["mix"], wo_ref, gpost_ref, gpre2_ref)
        st["acc"] = jnp.zeros(x_ref.shape, f32)

    def up_stage(c):
        st["a", c] = _ffn_up(st["hb"], wup_ref, c)

    def down_stage(c):
        st["acc"] = _ffn_down(st["acc"], st.pop(("a", c)), wdn_ref, c)

    def final_stage():
        out_ref[...] = st["x1"] + _rms(st["acc"], gpost2_ref[...])

    n_chunks = wup_ref.shape[1] // FF_CHUNK
    pairs = [[functools.partial(up_stage, c), functools.partial(down_stage, c - 1)] for c in range(1, n_chunks)]
    share, extra = divmod(len(pairs), n_units - 1)
    stages = [[merge_stage, proj_stage, functools.partial(up_stage, 0)]]
    for u in range(n_units - 1):
        stages.append([f for _ in range(share + (1 if u < extra else 0)) for f in pairs.pop(0)])
    stages[-1] += [functools.partial(down_stage, n_chunks - 1), final_stage]

    def keys_unit(s, src_ref):
        q6 = jnp.concatenate([qa_ref[s]] * HEADS, axis=0)
        qbd = jnp.where(own, q6, 0.0).astype(bf16)
        st["p"], _, st["l"] = _cache_scores(qbd, src_ref, ka_ref[s], taba_ref)
        _shift_into(out_buf.at[0], src_ref, ka_ref[s])

    def values_unit(s, src_ref):
        res = jnp.where(own, _cache_values(st["p"], st["l"], src_ref, va_ref[s]), 0.0)
        oa = res[0:t, :]
        for h in range(1, HEADS):
            oa = oa + res[h * t:(h + 1) * t, :]
        oa_ref[s] = oa
        _shift_into(out_buf.at[1], src_ref, va_ref[s])

    @pl.when(step == 0)
    def _():
        for u in range(READ_AHEAD):
            read(u).start()

    for u in range(n_units):
        if u + READ_AHEAD < n_units:
            read(u + READ_AHEAD).start()
        else:
            @pl.when(step + 1 < n_steps)
            def _():
                read(u + READ_AHEAD).start()
        read(u).wait()
        if u >= 2:
            write(u - 2).wait()
        else:
            @pl.when(step > 0)
            def _():
                write(u).wait()
        for stage in stages[u]:
            stage()
        (keys_unit if u % 2 == 0 else values_unit)(u // 2, in_buf.at[read_slot(u)])
        write(u).start()

    @pl.when(step == n_steps - 1)
    def _():
        write(n_units - 2).wait()
        write(n_units - 1).wait()


def _prompt_finish(layer, x, row_inputs, sample_qkv, taba, params, caches, prev_outs):
    s, d = x.shape
    t = ROW_TILE
    n_steps = s // t
    n_seq = sample_qkv[0].shape[0]
    assert n_seq % n_steps == 0
    per_step = n_seq // n_steps
    row = lambda a: pl.BlockSpec((a.shape[0] * t // s, a.shape[1]), lambda i: (i, 0))
    seqs = lambda a: pl.BlockSpec((per_step,) + a.shape[1:], lambda i: (i, 0, 0))
    const = lambda a: pl.BlockSpec(a.shape, lambda i: (0, 0), pipeline_mode=pl.Buffered(1))
    gain = lambda a: pl.BlockSpec((None, 1, a.shape[2]), lambda i: (layer, 0, 0))
    w_out, g_post, g_pre2, w_up, w_down, g_post2 = params
    cak, cav = caches
    n_alias = 0 if prev_outs is None else len(prev_outs)
    any_spec = pl.BlockSpec(memory_space=pl.ANY)
    args = ([x] + row_inputs + list(sample_qkv)
            + [taba, w_out, g_post, g_pre2, w_up, w_down, g_post2, cak, cav]
            + (list(prev_outs) if n_alias else []))
    in_specs = ([row(a) for a in [x] + row_inputs] + [seqs(a) for a in sample_qkv]
                + [const(taba), _layer_block(w_out, layer), gain(g_post), gain(g_pre2),
                   _layer_block(w_up, layer), _layer_block(w_down, layer), gain(g_post2), any_spec, any_spec]
                + [any_spec] * n_alias)
    first_alias = len(args) - n_alias
    assert (2 * per_step) % WRITE_SLOTS == 0
    outs = pl.pallas_call(
        functools.partial(_prompt_finish_kernel, layer, n_alias),
        grid=(n_steps,),
        in_specs=in_specs,
        out_specs=[pl.BlockSpec((t, d), lambda i: (i, 0)), seqs(sample_qkv[0]), any_spec, any_spec],
        out_shape=[jax.ShapeDtypeStruct((s, d), f32), jax.ShapeDtypeStruct(sample_qkv[0].shape, f32),
                   jax.ShapeDtypeStruct(cak.shape, f32), jax.ShapeDtypeStruct(cav.shape, f32)],
        scratch_shapes=[pltpu.VMEM((A_WIDTH // LANES, t, LANES), f32),
                        pltpu.VMEM((GROUP * A_WIDTH // LANES, t // GROUP, LANES), f32),
                        pltpu.VMEM((READ_SLOTS,) + cak.shape[2:], f32), pltpu.VMEM((WRITE_SLOTS,) + cak.shape[2:], f32),
                        pltpu.SemaphoreType.DMA((READ_SLOTS,)), pltpu.SemaphoreType.DMA((WRITE_SLOTS,))],
        input_output_aliases={first_alias + i: 2 + i for i in range(n_alias)},
        compiler_params=pltpu.CompilerParams(dimension_semantics=("arbitrary",),
                                             vmem_limit_bytes=VMEM_LIMIT_STREAMING),
        name=f"prompt_finish_l{layer}",
    )(*args)
    return outs[0], outs[1], (outs[2], outs[3])


def _sample_finish_kernel(x_ref, mix_ref, wo_ref, gpost_ref, gpre2_ref, wup_ref, wdn_ref, gpost2_ref, out_ref):
    x1, hb = _out_proj_stage(x_ref[...], mix_ref[...].astype(bf16), wo_ref, gpost_ref, gpre2_ref)
    acc = jnp.zeros(x1.shape, f32)
    n_chunks = wup_ref.shape[1] // FF_CHUNK
    a2 = _ffn_up(hb, wup_ref, 0)
    for c in range(n_chunks):
        a2_next = _ffn_up(hb, wup_ref, c + 1) if c + 1 < n_chunks else None
        acc = _ffn_down(acc, a2, wdn_ref, c)
        a2 = a2_next
    out_ref[...] = x1 + _rms(acc, gpost2_ref[...])


def _sample_finish(layer, x, mix, params):
    s, d = x.shape
    t = ROW_TILE
    row = lambda a: pl.BlockSpec((t, a.shape[1]), lambda i: (i, 0))
    gain = lambda a: pl.BlockSpec((None, 1, a.shape[2]), lambda i: (layer, 0, 0))
    w_out, g_post, g_pre2, w_up, w_down, g_post2 = params
    return pl.pallas_call(
        _sample_finish_kernel,
        grid=(s // t,),
        in_specs=[row(x), row(mix), _layer_block(w_out, layer), gain(g_post), gain(g_pre2),
                  _layer_block(w_up, layer), _layer_block(w_down, layer), gain(g_post2)],
        out_specs=pl.BlockSpec((t, d), lambda i: (i, 0)),
        out_shape=jax.ShapeDtypeStruct((s, d), f32),
        compiler_params=pltpu.CompilerParams(dimension_semantics=("arbitrary",),
                                             vmem_limit_bytes=VMEM_LIMIT),
        name="sample_finish",
    )(x, mix, w_out, g_post, g_pre2, w_up, w_down, g_post2)


def _sample_proj_kernel(x_ref, g_ref, w_ref, qa_ref, ka_ref, va_ref, qb_ref, kb_ref, vb_ref, cb_ref, u_ref):
    aq, ak, av, bq, bk, bv, cb, cc, cx = _project(x_ref, g_ref, w_ref)
    qa_ref[...] = aq * SCALE
    ka_ref[...] = ak
    va_ref[...] = av
    qb_ref[...] = bq * SCALE
    kb_ref[...] = bk
    vb_ref[...] = bv
    cb_ref[...] = cb
    u_ref[...] = cc * cx


def _sample_proj(layer, x, g, w):
    s, d = x.shape
    t = ROW_TILE
    widths = (A_WIDTH, A_WIDTH, A_WIDTH, A_WIDTH, B_KV_WIDTH, B_KV_WIDTH, C_WIDTH, C_WIDTH)
    row = lambda width: pl.BlockSpec((t, width), lambda i: (i, 0))
    const = lambda shape: pl.BlockSpec(shape, lambda i: (0, 0))
    return pl.pallas_call(
        _sample_proj_kernel,
        grid=(s // t,),
        in_specs=[row(d), const((1, d)), _layer_block(w, layer)],
        out_specs=[row(wd) for wd in widths],
        out_shape=[jax.ShapeDtypeStruct((s, wd), f32) for wd in widths],
        compiler_params=pltpu.CompilerParams(dimension_semantics=("arbitrary",),
                                             vmem_limit_bytes=VMEM_LIMIT),
        name="sample_proj",
    )(x, g, w)


def _sample_rest_kernel(n_alias, sink_ref, oa_ref, qb_ref, kb_ref, vb_ref, cb_ref, u_ref,
                        cbk_ref, cbv_ref, cst_ref, tabb_ref, cw_ref, *rest):
    mix_ref, nbk_ref, nbv_ref, nst_ref, uext_s = rest[n_alias:]
    t = qb_ref.shape[1]
    rows = HEADS * t
    low = _low_half((t, LANES))
    zero = jnp.zeros((t, LANES), f32)
    row_head = lax.broadcasted_iota(jnp.int32, (rows, 1), 0) >> (t.bit_length() - 1)
    sink_col = jnp.zeros((rows, 1), f32)
    for h in range(HEADS):
        sink_col = jnp.where(row_head == h, sink_ref[h], sink_col)
    cw = cw_ref[...]
    n_seq = qb_ref.shape[0]
    scores = []
    for g in range(n_seq):
        qb = qb_ref[g]
        t0, t1, t2 = (qb[:, i * LANES:(i + 1) * LANES] for i in range(3))
        qbd = jnp.concatenate([
            jnp.where(low, t0, zero), jnp.where(low, _swap_halves(t0), zero), jnp.where(low, t1, zero),
            jnp.where(low, zero, t1), jnp.where(low, zero, _swap_halves(t2)), jnp.where(low, zero, t2)],
            axis=0).astype(bf16)
        scores.append(_cache_scores(qbd, cbk_ref.at[g], kb_ref[g], tabb_ref))
    outs = [_cache_values(pb, lb, cbv_ref.at[g], vb_ref[g]) for g, (pb, _, lb) in enumerate(scores)]
    for g in range(n_seq):
        _, mb, lb = scores[g]
        resb = outs[g] * jax.nn.sigmoid(mb + jnp.log(lb) - sink_col)
        r = [resb[h * t:(h + 1) * t, :] for h in range(HEADS)]
        ob = jnp.concatenate([jnp.where(low, r[0], _swap_halves(r[1])), jnp.where(low, r[2], r[3]),
                              jnp.where(low, _swap_halves(r[4]), r[5])], axis=1)
        _shift_into(nbk_ref.at[g], cbk_ref.at[g], kb_ref[g])
        _shift_into(nbv_ref.at[g], cbv_ref.at[g], vb_ref[g])

        u = u_ref[g]
        uext_s[g, SUBLANES - (CONV_WIDTH - 1):SUBLANES, :] = cst_ref[g]
        uext_s[g, SUBLANES:, :] = u
        u1 = uext_s[g, pl.ds(SUBLANES - 1, t), :]
        u2 = uext_s[g, pl.ds(SUBLANES - 2, t), :]
        oc = cb_ref[g] * (cw[0:1, :] * u2 + cw[1:2, :] * u1 + cw[2:3, :] * u)
        nst_ref[g] = uext_s[g, pl.ds(SUBLANES + t - (CONV_WIDTH - 1), CONV_WIDTH - 1), :]
        mix_ref[g] = jnp.concatenate([oa_ref[g], ob, oc], axis=1)


def _sample_rest(layer, sinks, oa, proj, caches, tabb, cw, prev_outs):
    cbk, cbv, cst = caches
    n, t = oa.shape[0], oa.shape[1]
    g = SEQS_PER_STEP
    proj3 = [a.reshape(n, t, a.shape[1]) for a in proj]
    per_seq = lambda a: pl.BlockSpec((g,) + a.shape[1:], lambda i: (i, 0, 0))
    per_layer_seq = lambda a: pl.BlockSpec((None, g) + a.shape[2:], lambda i: (layer, i, 0, 0))
    const = lambda a: pl.BlockSpec(a.shape, lambda i: (0, 0))
    n_alias = 0 if prev_outs is None else len(prev_outs)
    args = [sinks, oa] + proj3 + [cbk, cbv, cst, tabb, cw] + (list(prev_outs) if n_alias else [])
    in_specs = ([pl.BlockSpec(memory_space=pltpu.SMEM), per_seq(oa)] + [per_seq(a) for a in proj3]
                + [per_layer_seq(a) for a in caches] + [const(tabb), const(cw)]
                + [pl.BlockSpec(memory_space=pl.ANY)] * n_alias)
    first_alias = len(args) - n_alias
    outs = pl.pallas_call(
        functools.partial(_sample_rest_kernel, n_alias),
        grid=(n // g,),
        in_specs=in_specs,
        out_specs=[pl.BlockSpec((g, t, 4 * C_WIDTH), lambda i: (i, 0, 0)), per_layer_seq(cbk), per_layer_seq(cbv),
                   pl.BlockSpec((g, CONV_WIDTH - 1, C_WIDTH), lambda i: (i, 0, 0))],
        out_shape=[jax.ShapeDtypeStruct((n, t, 4 * C_WIDTH), f32), jax.ShapeDtypeStruct(cbk.shape, f32),
                   jax.ShapeDtypeStruct(cbv.shape, f32), jax.ShapeDtypeStruct((n, CONV_WIDTH - 1, C_WIDTH), f32)],
        scratch_shapes=[pltpu.VMEM((g, SUBLANES + t, C_WIDTH), f32)],
        input_output_aliases={first_alias + i: 1 + i for i in range(n_alias)},
        compiler_params=pltpu.CompilerParams(dimension_semantics=("arbitrary",),
                                             vmem_limit_bytes=VMEM_LIMIT),
        name=f"sample_rest_l{layer}",
    )(*args)
    mix, nbk, nbv, nst = outs
    return mix.reshape(n * t, 4 * C_WIDTH), (nbk, nbv), nst


def _t5_bucket(dist):
    dist = jnp.maximum(dist, 0)
    max_exact = NUM_BUCKETS // 2
    scaled = jnp.log(jnp.maximum(dist, 1).astype(f32) / max_exact) / math.log(MAX_DISTANCE / max_exact)
    large = max_exact + (scaled * (NUM_BUCKETS - max_exact)).astype(jnp.int32)
    large = jnp.minimum(large, NUM_BUCKETS - 1)
    return jnp.where(dist < max_exact, dist, large)


def _bias_of(table, dist):
    onehot = jax.nn.one_hot(_t5_bucket(dist), NUM_BUCKETS, dtype=f32)
    return jnp.einsum('...b,bh->h...', onehot, table.astype(f32), precision=lax.Precision.HIGHEST)


def _band_bias(table, d):
    qi = jnp.arange(BLOCK)[:, None]
    kj = jnp.arange(2 * BLOCK)[None, :]
    delta = BLOCK + qi - kj
    in_band = (delta >= 0) & (delta <= BLOCK)
    bias = _bias_of(table, delta * d)
    with_prev = jnp.where(in_band[None], bias, NEG)
    no_prev = jnp.where((in_band & (kj >= BLOCK))[None], bias, NEG)
    return jnp.stack([no_prev, with_prev], axis=0)


def _sample_table_a(table, t, buf, cols):
    i = jnp.arange(t)[:, None]
    r = jnp.arange(cols)[None, :]
    delta = buf + i - r
    count = sum(((delta >= 0) & (delta % d == 0) & (delta // d <= w // d)).astype(f32) for w, d in A_PATTERNS)
    tab = jnp.where((count > 0)[None], _bias_of(table, delta) + jnp.log(jnp.maximum(count, 1.0))[None], NEG)
    return tab.reshape(table.shape[1] * t, cols)


def _sample_table_b(table, t, buf, cols):
    i = jnp.arange(t)[:, None]
    r = jnp.arange(cols)[None, :]
    delta = buf + i - r
    valid = (delta >= 0) & (delta <= B_WINDOW)
    return jnp.where(valid[None], _bias_of(table, delta), NEG).reshape(table.shape[1] * t, cols)


def kernel(x_prompt, x_sample, cache_a_k, cache_a_v, cache_b_k, cache_b_v, state_conv, w_in, w_out, conv_w,
           b_sinks, rel_bias, g_mix_pre, g_mix_post, w_up, w_down, g_mlp_pre, g_mlp_post):
    depth = w_in.shape[0]
    batch, seq, d_model = x_prompt.shape
    n_dec, t_dec, _ = x_sample.shape
    buf_a, buf_b = cache_a_k.shape[2], cache_b_k.shape[2]
    assert batch == 1 and seq % (A_PATTERNS[-1][1] * BLOCK) == 0 and seq >= A_WINDOW
    assert buf_a == A_WINDOW and buf_b == B_WINDOW and t_dec == SUBLANES
    assert (n_dec * t_dec) % ROW_TILE == 0 and seq % ROW_TILE == 0 and n_dec % SEQS_PER_STEP == 0
    assert seq % PROJ_TILE == 0 and A_WINDOW % PROJ_TILE == 0
    assert tuple(d for _, d in A_PATTERNS) == (1, GROUP, GROUP * GROUP) and seq % (GROUP * GROUP * Q_TILE) == 0

    xp = x_prompt.reshape(seq, d_model)
    xs = x_sample.reshape(n_dec * t_dec, d_model)
    to_fm = lambda c: jnp.transpose(c, (0, 1, 3, 4, 2)).reshape(depth, n_dec, c.shape[3] * HEAD_DIM, c.shape[2])
    from_fm = lambda c, heads: jnp.transpose(c.reshape(c.shape[:2] + (heads, HEAD_DIM, c.shape[3])), (0, 1, 4, 2, 3))
    caches_a = (to_fm(cache_a_k), to_fm(cache_a_v))
    caches_b = (to_fm(cache_b_k), to_fm(cache_b_v), state_conv)
    table_a, table_b = rel_bias[:, :HEADS], rel_bias[:, HEADS:]
    band_a = [_band_bias(table_a, d) for _, d in A_PATTERNS]
    band_b = _band_bias(table_b, 1)
    tab_sa = _sample_table_a(table_a, t_dec, buf_a, buf_a + LANES)
    tab_sb = _sample_table_b(table_b, t_dec, buf_b, buf_b + LANES)
    row = lambda v: v.reshape(1, -1).astype(f32)
    gains = lambda v: v.reshape(depth, 1, -1).astype(f32)
    w_in_b = w_in.astype(bf16)
    params = (w_out.astype(bf16), gains(g_mix_post), gains(g_mlp_pre), w_up.astype(bf16), w_down.astype(bf16),
              gains(g_mlp_post))

    new_p = [[] for _ in range(5)]
    conv_s = []
    new_a = new_b = None
    for l in range(depth):
        cw = conv_w[l].astype(f32)
        sinks = b_sinks[l].astype(f32)

        (qa1, qa4, qa16, ka1, ka4, ka16, va1, va4, va16, qb, kbx, vbx, oc,
         kat, vat, kbt, vbt, ut) = _prompt_proj(l, xp, row(g_mix_pre[l]), w_in_b, cw)
        pats = [_band_attention(q, k, v, bias) for q, k, v, bias in
                ((qa1, ka1, va1, band_a[0]), (qa4, ka4, va4, band_a[1]), (qa16, ka16, va16, band_a[2]))]
        ob = _band_attention(qb, kbx, vbx, band_b, sinks=sinks)
        for lst, a in zip(new_p, (kat[None, None], vat[None, None], kbt[None, None], vbt[None, None],
                                  ut.reshape(1, CONV_WIDTH - 1, C_WIDTH))):
            lst.append(a)

        proj_s = _sample_proj(l, xs, row(g_mix_pre[l]), w_in_b)
        qkv_s = [a.reshape(n_dec, t_dec, A_WIDTH) for a in proj_s[:3]]
        xp, oa_s, new_a = _prompt_finish(l, xp, [p[0] for p in pats] + [p[1] for p in pats] + [ob, oc], qkv_s,
                                         tab_sa, params, caches_a, new_a)

        mix_s, new_b, nst = _sample_rest(l, sinks, oa_s, proj_s[3:], caches_b, tab_sb, cw, new_b)
        conv_s.append(nst)
        xs = _sample_finish(l, xs, mix_s, params)

    a_k_p, a_v_p, b_k_p, b_v_p = [jnp.concatenate(t, axis=0) for t in new_p[:4]]
    return (xp.reshape(batch, seq, d_model), xs.reshape(n_dec, t_dec, d_model),
            from_fm(a_k_p, HEADS), from_fm(a_v_p, HEADS), from_fm(b_k_p, 2), from_fm(b_v_p, 2),
            jnp.stack(new_p[4], axis=0),
            from_fm(new_a[0], HEADS), from_fm(new_a[1], HEADS), from_fm(new_b[0], 2), from_fm(new_b[1], 2),
            jnp.stack(conv_s, axis=0))
```
